```python
import jax
import jax.numpy as jnp
from jax import lax
import numpy as np

D_MODEL = 2048
BATCH = 4
SEQ = 8192
DEPTH = 4
DEC_BATCH = 4
DEC_SEQ = 2048
PAST_LEN = 128

N_EVEN = (DEPTH + 1) // 2
N_ODD = DEPTH // 2
N_SUB = 3
EPS = 1e-6
RES_HALF = 0.5

FFN_DIM = 5632

HG_HEADS = 8
HG_DK = 128
HG_DV = 128
HG_WIDTH = HG_HEADS * HG_DK

MLA_HEADS = 8
MLA_NOPE = 128
MLA_ROPE = 64
MLA_QK = MLA_NOPE + MLA_ROPE
MLA_V = 128
Q_LORA = 512
KV_LORA = 512
ROPE_THETA = 10000.0
Q_BLOCK = 128

GLA_HEADS = 4
GLA_DK = 256
GLA_DV = 512
GLA_GATE_RANK = 16
GLA_GATE_NORM = 16.0

CHUNK = 64

EV_IN = 5 * HG_WIDTH + Q_LORA + KV_LORA + MLA_ROPE
EV_MIX = HG_HEADS * HG_DV + MLA_HEADS * MLA_V
OD_IN = 2 * GLA_HEADS * GLA_DK + 2 * GLA_HEADS * GLA_DV + 2 * GLA_GATE_RANK
OD_MIX = GLA_HEADS * GLA_DV

kernel_name = 'hybrid_bidir_hgrn2_mla_gla_encoder'


def rms_norm(x, g):
    xf = x.astype(jnp.float32)
    y = xf * lax.rsqrt(jnp.mean(xf * xf, axis=-1, keepdims=True) + EPS)
    return (y * g.astype(jnp.float32)).astype(x.dtype)


def swiglu(h, w13, w2):
    g, u = jnp.split(h @ w13, 2, axis=-1)
    return (jax.nn.silu(g) * u) @ w2


def split_heads(a, n_heads):
    B, T, _ = a.shape
    return a.reshape(B, T, n_heads, -1).transpose(0, 2, 1, 3)


def merge_heads(a):
    B, H, T, d = a.shape
    return a.transpose(0, 2, 1, 3).reshape(B, T, H * d)


def apply_rope(x, pos):
    half = MLA_ROPE // 2
    inv_freq = ROPE_THETA ** (-jnp.arange(half, dtype=jnp.float32) / half)
    ang = pos.astype(jnp.float32)[:, None] * inv_freq[None, :]
    cos, sin = jnp.cos(ang), jnp.sin(ang)
    xf = x.astype(jnp.float32)
    x1, x2 = xf[..., :half], xf[..., half:]
    return jnp.concatenate([x1 * cos - x2 * sin, x1 * sin + x2 * cos], axis=-1).astype(x.dtype)


def gated_scan(q, k, v, log_f):
    B, H, T, dk = q.shape
    dv = v.shape[-1]
    n = T // CHUNK

    def chunks(a):
        return a.reshape(B, H, n, CHUNK, a.shape[-1]).transpose(2, 0, 1, 3, 4)

    tri = jnp.tril(jnp.ones((CHUNK, CHUNK), dtype=bool))[:, :, None]

    def step(S, inp):
        qc, kc, vc, gc = inp
        b = jnp.cumsum(gc, axis=2)
        diff = jnp.where(tri, b[:, :, :, None, :] - b[:, :, None, :, :], -jnp.inf)
        att = jnp.einsum('bhid,bhjd,bhijd->bhij', qc, kc, jnp.exp(diff))
        o = jnp.einsum('bhij,bhje->bhie', att, vc) + jnp.einsum('bhid,bhde->bhie', qc * jnp.exp(b), S)
        b_last = b[:, :, -1:, :]
        S = jnp.exp(b_last[:, :, 0, :, None]) * S + jnp.einsum('bhjd,bhje->bhde', kc * jnp.exp(b_last - b), vc)
        return S, o

    S0 = jnp.zeros((B, H, dk, dv), jnp.float32)
    _, o = lax.scan(step, S0, (chunks(q), chunks(k), chunks(v), chunks(log_f)))
    return o.transpose(1, 2, 0, 3, 4).reshape(B, H, T, dv)


def bidirectional_scan(q, k_fwd, k_bwd, v, lf_fwd, lf_bwd):
    rev = lambda a: jnp.flip(a, axis=2)
    fwd = gated_scan(q, k_fwd, v, lf_fwd)
    bwd = rev(gated_scan(rev(q), rev(k_bwd), rev(v), rev(lf_bwd)))
    return fwd + bwd


def hgrn2_mixer(u, lb_fwd, lb_bwd, onorm_g):
    f32 = jnp.float32
    q, z_fwd, z_bwd, i, g = jnp.split(u, 5, axis=-1)
    q = jax.nn.silu(split_heads(q, HG_HEADS).astype(f32))
    v = split_heads(i, HG_HEADS).astype(f32)

    def forget(z, lb):
        z = split_heads(z, HG_HEADS).astype(f32)
        lb = lb.reshape(HG_HEADS, 1, HG_DK)
        log_f = jnp.logaddexp(jnp.log(lb), jnp.log1p(-lb) + jax.nn.log_sigmoid(z))
        k = (1.0 - lb) * jax.nn.sigmoid(-z)
        return k, log_f

    k_fwd, lf_fwd = forget(z_fwd, lb_fwd)
    k_bwd, lf_bwd = forget(z_bwd, lb_bwd)
    o = bidirectional_scan(q, k_fwd, k_bwd, v, lf_fwd, lf_bwd)
    o = rms_norm(o, onorm_g) * jax.nn.silu(split_heads(g, HG_HEADS).astype(f32))
    return merge_heads(o).astype(u.dtype)


def block_attention(q, k, v):
    B, H, T, dq = q.shape
    nb = T // Q_BLOCK
    scale = dq ** -0.5
    qb = q.reshape(B, H, nb, Q_BLOCK, dq).transpose(2, 0, 1, 3, 4)

    def one_block(qi):
        s = jnp.einsum('bhqd,bhkd->bhqk', qi, k).astype(jnp.float32) * scale
        p = jax.nn.softmax(s, axis=-1).astype(v.dtype)
        return jnp.einsum('bhqk,bhkd->bhqd', p, v)

    o = lax.map(one_block, qb)
    return o.transpose(1, 2, 0, 3, 4).reshape(B, H, T, v.shape[-1])


def mla_mixer(u, pos, qa_norm_g, w_uq, kva_norm_g, w_ukv, qn_g, kn_g):
    B, T, _ = u.shape
    c_q = u[..., :Q_LORA]
    c_kv = u[..., Q_LORA:Q_LORA + KV_LORA]
    k_pe = u[..., Q_LORA + KV_LORA:]
    q = split_heads(rms_norm(c_q, qa_norm_g) @ w_uq, MLA_HEADS)
    kv = split_heads(rms_norm(c_kv, kva_norm_g) @ w_ukv, MLA_HEADS)
    k_nope, v = kv[..., :MLA_NOPE], kv[..., MLA_NOPE:]
    k_pe = jnp.broadcast_to(k_pe[:, None], (B, MLA_HEADS, T, MLA_ROPE))
    k = jnp.concatenate([k_nope, k_pe], axis=-1)
    q = rms_norm(q, qn_g)
    k = rms_norm(k, kn_g)
    q = jnp.concatenate([q[..., :MLA_NOPE], apply_rope(q[..., MLA_NOPE:], pos)], axis=-1)
    k = jnp.concatenate([k[..., :MLA_NOPE], apply_rope(k[..., MLA_NOPE:], pos)], axis=-1)
    return merge_heads(block_attention(q, k, v))


def gla_mixer(u, gk_w2, gk_b, onorm_g):
    f32 = jnp.float32
    kw = GLA_HEADS * GLA_DK
    vw = GLA_HEADS * GLA_DV
    q, k, v, g, r_fwd, r_bwd = jnp.split(u, [kw, 2 * kw, 2 * kw + vw, 2 * kw + 2 * vw, 2 * kw + 2 * vw + GLA_GATE_RANK], axis=-1)
    q = split_heads(q, GLA_HEADS).astype(f32) * (GLA_DK ** -0.5)
    k = split_heads(k, GLA_HEADS).astype(f32)
    v = split_heads(v, GLA_HEADS).astype(f32)

    def log_gate(r, d):
        z = (r @ gk_w2[d] + gk_b[d]).astype(f32)
        return split_heads(jax.nn.log_sigmoid(z) / GLA_GATE_NORM, GLA_HEADS)

    o = bidirectional_scan(q, k, k, v, log_gate(r_fwd, 0), log_gate(r_bwd, 1))
    o = rms_norm(o, onorm_g) * jax.nn.silu(split_heads(g, GLA_HEADS).astype(f32))
    return merge_heads(o).astype(u.dtype)


def trunk(x, c, w):
    B, T, _ = x.shape
    pos = jnp.arange(T, dtype=jnp.int32)
    p = jax.nn.softmax(w['hgrn_lb'].astype(jnp.float32), axis=1)
    lb = jnp.cumsum(p, axis=1)
    lb = lb - lb[:, :1]
    cond = jax.nn.silu(c)
    for l in range(DEPTH):
        mod = (cond @ w['ada_w'][l] + w['ada_b'][l]).reshape(B, 1, N_SUB, 3, D_MODEL)
        shift, scale, gate = mod[:, :, :, 0], mod[:, :, :, 1], mod[:, :, :, 2]

        def adaln(x, s):
            return rms_norm(x, w['norm_g'][l, s]) * (1.0 + scale[:, :, s]) + shift[:, :, s]

        h = adaln(x, 0)
        x = x + RES_HALF * gate[:, :, 0] * swiglu(h, w['ffn_w13'][l, 0], w['ffn_w2'][l, 0])
        h = adaln(x, 1)
        e = l // 2
        if l % 2 == 0:
            u = h @ w['ev_w_in'][e]
            u_hg, u_mla = u[..., :5 * HG_WIDTH], u[..., 5 * HG_WIDTH:]
            o_hg = hgrn2_mixer(u_hg, lb[0, e], lb[1, e], w['hgrn_onorm_g'][e])
            o_mla = mla_mixer(u_mla, pos, w['mla_qa_norm_g'][e], w['mla_w_uq'][e], w['mla_kva_norm_g'][e],
                              w['mla_w_ukv'][e], w['mla_qn_g'][e], w['mla_kn_g'][e])
            y = jnp.concatenate([o_hg, o_mla], axis=-1) @ w['ev_w_out'][e]
        else:
            u = h @ w['od_w_in'][e]
            y = gla_mixer(u, w['gla_gk_w2'][e], w['gla_gk_b'][e], w['gla_onorm_g'][e]) @ w['od_w_out'][e]
        x = x + gate[:, :, 1] * y
        h = adaln(x, 2)
        x = x + RES_HALF * gate[:, :, 2] * swiglu(h, w['ffn_w13'][l, 1], w['ffn_w2'][l, 1])
    return x


def setup_inputs(seed: int = 0) -> dict:
    key = jax.random.key(seed)
    ks = jax.random.split(key, 24)
    f32 = jnp.float32
    nrm = lambda k, shape, s: jax.random.normal(k, shape, f32) * s
    gain = lambda k, shape: 1.0 + 0.05 * jax.random.normal(k, shape, f32)
    return {
        'x_prompt': nrm(ks[0], (BATCH, SEQ, D_MODEL), 1.0),
        'x_sample': nrm(ks[1], (DEC_BATCH, DEC_SEQ, D_MODEL), 1.0),
        'c_prompt': nrm(ks[2], (BATCH, D_MODEL), 1.0),
        'c_sample': nrm(ks[3], (DEC_BATCH, D_MODEL), 1.0),
        'ada_w': nrm(ks[4], (DEPTH, D_MODEL, 3 * N_SUB * D_MODEL), 0.5 * D_MODEL ** -0.5),
        'ada_b': nrm(ks[5], (DEPTH, 3 * N_SUB * D_MODEL), 0.02),
        'norm_g': gain(ks[6], (DEPTH, N_SUB, D_MODEL)),
        'ffn_w13': nrm(ks[7], (DEPTH, 2, D_MODEL, 2 * FFN_DIM), D_MODEL ** -0.5),
        'ffn_w2': nrm(ks[8], (DEPTH, 2, FFN_DIM, D_MODEL), FFN_DIM ** -0.5),
        'ev_w_in': nrm(ks[9], (N_EVEN, D_MODEL, EV_IN), D_MODEL ** -0.5),
        'ev_w_out': nrm(ks[10], (N_EVEN, EV_MIX, D_MODEL), EV_MIX ** -0.5),
        'hgrn_lb': nrm(ks[11], (2, N_EVEN, HG_WIDTH), 0.5),
        'hgrn_onorm_g': gain(ks[12], (N_EVEN, HG_DV)),
        'mla_qa_norm_g': gain(ks[13], (N_EVEN, Q_LORA)),
        'mla_w_uq': nrm(ks[14], (N_EVEN, Q_LORA, MLA_HEADS * MLA_QK), Q_LORA ** -0.5),
        'mla_kva_norm_g': gain(ks[15], (N_EVEN, KV_LORA)),
        'mla_w_ukv': nrm(ks[16], (N_EVEN, KV_LORA, MLA_HEADS * (MLA_NOPE + MLA_V)), KV_LORA ** -0.5),
        'mla_qn_g': gain(ks[17], (N_EVEN, MLA_QK)),
        'mla_kn_g': gain(ks[18], (N_EVEN, MLA_QK)),
        'od_w_in': nrm(ks[19], (N_ODD, D_MODEL, OD_IN), D_MODEL ** -0.5),
        'od_w_out': nrm(ks[20], (N_ODD, OD_MIX, D_MODEL), OD_MIX ** -0.5),
        'gla_gk_w2': nrm(ks[21], (N_ODD, 2, GLA_GATE_RANK, GLA_HEADS * GLA_DK), GLA_GATE_RANK ** -0.5),
        'gla_gk_b': nrm(ks[22], (N_ODD, 2, GLA_HEADS * GLA_DK), 0.1),
        'gla_onorm_g': gain(ks[23], (N_ODD, GLA_DV)),
    }


def reference(x_prompt, x_sample, c_prompt, c_sample, ada_w, ada_b, norm_g, ffn_w13, ffn_w2,
              ev_w_in, ev_w_out, hgrn_lb, hgrn_onorm_g, mla_qa_norm_g, mla_w_uq, mla_kva_norm_g,
              mla_w_ukv, mla_qn_g, mla_kn_g, od_w_in, od_w_out, gla_gk_w2, gla_gk_b, gla_onorm_g):
    w = {
        'ada_w': ada_w, 'ada_b': ada_b, 'norm_g': norm_g, 'ffn_w13': ffn_w13, 'ffn_w2': ffn_w2,
        'ev_w_in': ev_w_in, 'ev_w_out': ev_w_out, 'hgrn_lb': hgrn_lb, 'hgrn_onorm_g': hgrn_onorm_g,
        'mla_qa_norm_g': mla_qa_norm_g, 'mla_w_uq': mla_w_uq, 'mla_kva_norm_g': mla_kva_norm_g,
        'mla_w_ukv': mla_w_ukv, 'mla_qn_g': mla_qn_g, 'mla_kn_g': mla_kn_g,
        'od_w_in': od_w_in, 'od_w_out': od_w_out, 'gla_gk_w2': gla_gk_w2, 'gla_gk_b': gla_gk_b,
        'gla_onorm_g': gla_onorm_g,
    }
    y_prompt = trunk(x_prompt, c_prompt, w)
    y_sample = trunk(x_sample, c_sample, w)
    return (y_prompt, y_sample)
```

```python
import functools
import math

import numpy as np
import jax
import jax.numpy as jnp
from jax import lax
from jax.experimental import pallas as pl
from jax.experimental.pallas import tpu as pltpu

F32 = jnp.float32
BF16 = jnp.bfloat16

D_MODEL = 2048
DEPTH = 4
N_SUB = 3
EPS = 1e-6
FFN_DIM = 5632

HG_HEADS = 8
HG_D = 128
HG_WIDTH = HG_HEADS * HG_D

MLA_HEADS = 8
MLA_NOPE = 128
MLA_ROPE = 64
MLA_QK = MLA_NOPE + MLA_ROPE
MLA_V = 128
MLA_QK_PAD = 256
Q_LORA = 512
KV_LORA = 512
MLA_C_PAD = 1152
ROPE_THETA = 10000.0

GLA_HEADS = 4
GLA_DK = 256
GLA_DV = 512
GLA_RANK = 16
GLA_GATE_NORM = 16.0
GLA_MAIN = 2 * GLA_HEADS * GLA_DK + 2 * GLA_HEADS * GLA_DV
GLA_R_PAD = 128

CHUNK = 64
LANE = 128
VMEM_LIMIT = 56 * 1024 * 1024

TOKEN_TILE = 512
FFN_TILE = 512
SCAN_BLOCK = 256
ATTN_TQ = 256
ATTN_TK = 512


def _cparams(sem):
    return pltpu.CompilerParams(dimension_semantics=sem, vmem_limit_bytes=VMEM_LIMIT)


def _silu(x):
    return x * jax.nn.sigmoid(x)


def _log_sigmoid(z):
    return jnp.minimum(z, 0.0) - jnp.log1p(jnp.exp(-jnp.abs(z)))


def _adaln(x, g, scale, shift):
    ms = jnp.mean(x * x, axis=-1, keepdims=True)
    y = x * lax.rsqrt(ms + EPS) * g
    return y * (1.0 + scale) + shift


def _dot(a, b):
    return jnp.dot(a, b, preferred_element_type=F32)


def _dot_nt(a, b):
    return lax.dot_general(a, b, (((1,), (1,)), ((), ())), preferred_element_type=F32)


def _dot_tn(a, b):
    return lax.dot_general(a, b, (((0,), (0,)), ((), ())), preferred_element_type=F32)


def _ada_kernel(c_ref, w_ref, b_ref, o_ref):
    cond = _silu(c_ref[...]).astype(BF16)
    o_ref[0] = _dot(cond, w_ref[0].astype(BF16)) + b_ref[0]


def _ada_mod(c_all, ada_w, ada_b):
    nb = c_all.shape[0]
    n_out = ada_w.shape[-1]
    tn = 1024
    return pl.pallas_call(
        _ada_kernel,
        grid=(DEPTH, n_out // tn),
        in_specs=[
            pl.BlockSpec((nb, D_MODEL), lambda l, j: (0, 0)),
            pl.BlockSpec((1, D_MODEL, tn), lambda l, j: (l, 0, j)),
            pl.BlockSpec((1, 1, tn), lambda l, j: (l, 0, j)),
        ],
        out_specs=pl.BlockSpec((1, nb, tn), lambda l, j: (l, 0, j)),
        out_shape=jax.ShapeDtypeStruct((DEPTH, nb, n_out), F32),
        compiler_params=_cparams(("arbitrary", "arbitrary")),
        name="ada_mod",
    )(c_all, ada_w, ada_b.reshape(DEPTH, 1, n_out))


class _Mod:
    def __init__(self, table, norm_g, nb_total, boff, tiles_per_seq):
        self.table = table
        self.norm_g = norm_g
        self.nb_total = nb_total
        self.boff = boff
        self.tps = tiles_per_seq

    def spec(self, layer, sub, kind):
        nb, boff, tps = self.nb_total, self.boff, self.tps
        return pl.BlockSpec(
            (1, 1, D_MODEL),
            lambda i, *_: (((layer * nb + boff + i // tps) * 9 + sub * 3 + kind), 0, 0))

    def norm_spec(self, layer, sub):
        return pl.BlockSpec((1, 1, D_MODEL), lambda i, *_: (layer * N_SUB + sub, 0, 0))


def _ffn_kernel(x_ref, sh_ref, sc_ref, gt_ref, ng_ref, w1_ref, w3_ref, w2_ref, o_ref, h_ref, acc_ref, *, nj):
    j = pl.program_id(1)

    @pl.when(j == 0)
    def _():
        h_ref[...] = _adaln(x_ref[...], ng_ref[0], sc_ref[0], sh_ref[0]).astype(BF16)

    h = h_ref[...]
    a = _dot(h, w1_ref[...])
    u = _dot(h, w3_ref[...])
    act = (_silu(a) * u).astype(BF16)
    contrib = _dot(act, w2_ref[...])

    @pl.when(j == 0)
    def _():
        acc_ref[...] = contrib

    @pl.when(j > 0)
    def _():
        acc_ref[...] += contrib

    @pl.when(j == nj - 1)
    def _():
        o_ref[...] = x_ref[...] + (0.5 * gt_ref[0]) * acc_ref[...]


def _ffn(x, mod, layer, sub, which, w13, w2):
    n = x.shape[0]
    tm, tf = TOKEN_TILE, FFN_TILE
    nj = FFN_DIM // tf
    return pl.pallas_call(
        functools.partial(_ffn_kernel, nj=nj),
        grid=(n // tm, nj),
        in_specs=[
            pl.BlockSpec((tm, D_MODEL), lambda i, j: (i, 0)),
            mod.spec(layer, sub, 0), mod.spec(layer, sub, 1), mod.spec(layer, sub, 2),
            mod.norm_spec(layer, sub),
            pl.BlockSpec((None, None, D_MODEL, tf), lambda i, j: (layer, which, 0, j)),
            pl.BlockSpec((None, None, D_MODEL, tf), lambda i, j: (layer, which, 0, j + nj)),
            pl.BlockSpec((None, None, tf, D_MODEL), lambda i, j: (layer, which, j, 0)),
        ],
        out_specs=pl.BlockSpec((tm, D_MODEL), lambda i, j: (i, 0)),
        out_shape=jax.ShapeDtypeStruct((n, D_MODEL), F32),
        scratch_shapes=[pltpu.VMEM((tm, D_MODEL), BF16), pltpu.VMEM((tm, D_MODEL), F32)],
        compiler_params=_cparams(("arbitrary", "arbitrary")),
        name="ffn",
    )(x, mod.table, mod.table, mod.table, mod.norm_g, w13, w13, w2)


def _inproj_kernel(x_ref, sh_ref, sc_ref, ng_ref, wm_ref, ws_ref, om_ref, os_ref, h_ref):
    @pl.when(pl.program_id(1) == 0)
    def _():
        h = _adaln(x_ref[...], ng_ref[0], sc_ref[0], sh_ref[0]).astype(BF16)
        h_ref[...] = h
        os_ref[...] = _dot(h, ws_ref[...])

    om_ref[...] = _dot(h_ref[...], wm_ref[...])


def _inproj(x, mod, layer, w_main, w_small, tn):
    n = x.shape[0]
    tm = TOKEN_TILE
    n_main, n_small = w_main.shape[1], w_small.shape[1]
    return pl.pallas_call(
        _inproj_kernel,
        grid=(n // tm, n_main // tn),
        in_specs=[
            pl.BlockSpec((tm, D_MODEL), lambda i, j: (i, 0)),
            mod.spec(layer, 1, 0), mod.spec(layer, 1, 1),
            mod.norm_spec(layer, 1),
            pl.BlockSpec((D_MODEL, tn), lambda i, j: (0, j)),
            pl.BlockSpec((D_MODEL, n_small), lambda i, j: (0, 0)),
        ],
        out_specs=[
            pl.BlockSpec((tm, tn), lambda i, j: (i, j)),
            pl.BlockSpec((tm, n_small), lambda i, j: (i, 0)),
        ],
        out_shape=[jax.ShapeDtypeStruct((n, n_main), F32), jax.ShapeDtypeStruct((n, n_small), F32)],
        scratch_shapes=[pltpu.VMEM((tm, D_MODEL), BF16)],
        compiler_params=_cparams(("arbitrary", "arbitrary")),
        name="inproj",
    )(x, mod.table, mod.table, mod.norm_g, w_main, w_small)


def _outproj_kernel(x_ref, gt_ref, *refs):
    o_ref = refs[-1]
    npair = (len(refs) - 1) // 2
    y = _dot(refs[0][...], refs[npair][...])
    for p in range(1, npair):
        y = y + _dot(refs[p][...], refs[npair + p][...])
    o_ref[...] = x_ref[...] + gt_ref[0] * y


def _outproj(x, mod, layer, mixes, ws):
    n = x.shape[0]
    tm = TOKEN_TILE
    in_specs = [pl.BlockSpec((tm, D_MODEL), lambda i: (i, 0)), mod.spec(layer, 1, 2)]
    in_specs += [pl.BlockSpec((tm, m.shape[1]), lambda i: (i, 0)) for m in mixes]
    in_specs += [pl.BlockSpec(w.shape, lambda i: (0, 0)) for w in ws]
    return pl.pallas_call(
        _outproj_kernel,
        grid=(n // tm,),
        in_specs=in_specs,
        out_specs=pl.BlockSpec((tm, D_MODEL), lambda i: (i, 0)),
        out_shape=jax.ShapeDtypeStruct((n, D_MODEL), F32),
        compiler_params=_cparams(("arbitrary",)),
        name="outproj",
    )(x, mod.table, *mixes, *ws)


def _scan_consts(c):
    nlev = int(math.log2(c))
    t = np.arange(c)
    row, col = t[:, None], t[None, :]
    blocks = [col <= row, col > row]
    masks = [np.eye(c, dtype=bool)]
    for lev in range(nlev):
        s = 1 << lev
        blk = t // s
        odd = (blk % 2) == 1
        bstart = (blk * s)[:, None]
        bend = bstart + s - 1
        as_query = (col >= bstart) & (col <= row)
        as_key = (col > row) & (col <= bend)
        blocks.append(np.where(odd[:, None], as_query, as_key))
        masks.append(odd[:, None] & (~odd[None, :]) & ((row // (2 * s)) == (col // (2 * s))))
    ones = np.ones((8, c), dtype=bool)
    m_f = np.concatenate(blocks + [ones], axis=0)
    m_b = np.concatenate([b[::-1, ::-1] for b in blocks] + [ones], axis=0)
    msum = jnp.asarray(np.stack([m_f, m_b]).astype(np.float32), dtype=BF16)
    mask = jnp.asarray(np.stack([np.stack(masks), np.stack([m[::-1, ::-1] for m in masks])]).astype(np.float32))
    return msum, mask, nlev


def _split3(g):
    hi = g.astype(BF16)
    r1 = g - hi.astype(F32)
    mid = r1.astype(BF16)
    lo = (r1 - mid.astype(F32)).astype(BF16)
    return hi, mid, lo


def _chunk_core(q, k, v, g, msum, mask_ref, st, *, c, nlev):
    dk = q.shape[-1]
    hi, mid, lo = _split3(g)
    x3 = _dot(msum, jnp.concatenate([hi, mid, lo], axis=-1))
    x = x3[:, :dk] + x3[:, dk:2 * dk] + x3[:, 2 * dk:]
    e = jnp.exp(x)
    qe = (q * e[0:c]).astype(BF16)
    ke = (k * e[c:2 * c]).astype(BF16)
    dtot = e[(2 + nlev) * c:(2 + nlev) * c + 1]
    att = _dot_nt(q.astype(BF16), k.astype(BF16)) * mask_ref[0, 0]
    for lev in range(nlev):
        es = e[(2 + lev) * c:(3 + lev) * c]
        att = att + _dot_nt((q * es).astype(BF16), (k * es).astype(BF16)) * mask_ref[0, 1 + lev]
    vb = v.astype(BF16)
    o = _dot(att.astype(BF16), vb) + _dot_nt(qe, st.astype(BF16))
    st_new = st * dtot + _dot_tn(vb, ke)
    return o, st_new


def _scan_finish(o_sum, onorm, gate):
    ms = jnp.mean(o_sum * o_sum, axis=-1, keepdims=True)
    return (o_sum * lax.rsqrt(ms + EPS) * onorm * _silu(gate)).astype(BF16)


def _hgrn_kernel(q_ref, z_ref, v_ref, gate_ref, lb_ref, on_ref, msum_ref, mask_ref, o_ref, ofwd_ref, st_ref,
                 *, nblk, tb, c, nlev):
    ph = pl.program_id(2)
    jb = pl.program_id(3)
    blk = jnp.where(ph == 0, jb, nblk - 1 - jb)
    nch = tb // c

    @pl.when(jb == 0)
    def _():
        st_ref[...] = jnp.zeros_like(st_ref)

    log_lb = lb_ref[0, 0, 0:1, :]
    log_1mlb = lb_ref[0, 0, 1:2, :]
    one_mlb = lb_ref[0, 0, 2:3, :]
    msum = msum_ref[0]
    onorm = on_ref[...]

    for ci in range(nch):
        cidx = jnp.where(ph == 0, ci, nch - 1 - ci)
        off = pl.multiple_of(cidx * c, c)
        rows = pl.ds(off, c)
        q = _silu(q_ref[0, rows, :])
        z = z_ref[0, rows, :]
        v = v_ref[0, rows, :]
        a = log_lb
        b = log_1mlb + _log_sigmoid(z)
        g = jnp.maximum(a, b) + jnp.log1p(jnp.exp(-jnp.abs(a - b)))
        k = one_mlb * jax.nn.sigmoid(-z)
        o, st_new = _chunk_core(q, k, v, g, msum, mask_ref, st_ref[...], c=c, nlev=nlev)
        st_ref[...] = st_new
        grow = pl.ds(pl.multiple_of(blk * tb + cidx * c, c), c)

        @pl.when(ph == 0)
        def _():
            ofwd_ref[grow, :] = o

        @pl.when(ph == 1)
        def _():
            o_ref[0, rows, :] = _scan_finish(ofwd_ref[grow, :] + o, onorm, gate_ref[0, rows, :])


def _hgrn_scan(u, lbp, onorm, b, t):
    tb, c = min(SCAN_BLOCK, t), CHUNK
    nblk = t // tb
    msum, mask, nlev = _scan_consts(c)
    nh = HG_HEADS

    def blk_of(ph, jb):
        return jnp.where(ph == 0, jb, nblk - 1 - jb)

    return pl.pallas_call(
        functools.partial(_hgrn_kernel, nblk=nblk, tb=tb, c=c, nlev=nlev),
        grid=(b, nh, 2, nblk),
        in_specs=[
            pl.BlockSpec((1, tb, HG_D), lambda bi, h, ph, jb: (bi, blk_of(ph, jb), h)),
            pl.BlockSpec((1, tb, HG_D), lambda bi, h, ph, jb: (bi, blk_of(ph, jb), nh + nh * ph + h)),
            pl.BlockSpec((1, tb, HG_D), lambda bi, h, ph, jb: (bi, blk_of(ph, jb), 3 * nh + h)),
            pl.BlockSpec((1, tb, HG_D),
                         lambda bi, h, ph, jb: (bi, jnp.where(ph == 0, nblk - 1, nblk - 1 - jb), 4 * nh + h)),
            pl.BlockSpec((1, 1, 8, HG_D), lambda bi, h, ph, jb: (ph, h, 0, 0)),
            pl.BlockSpec((1, HG_D), lambda bi, h, ph, jb: (0, 0)),
            pl.BlockSpec((1,) + msum.shape[1:], lambda bi, h, ph, jb: (ph, 0, 0)),
            pl.BlockSpec((1,) + mask.shape[1:], lambda bi, h, ph, jb: (ph, 0, 0, 0)),
        ],
        out_specs=pl.BlockSpec(
            (1, tb, HG_D), lambda bi, h, ph, jb: (bi, jnp.where(ph == 0, nblk - 1, nblk - 1 - jb), h)),
        out_shape=jax.ShapeDtypeStruct((b, t, HG_WIDTH), BF16),
        scratch_shapes=[pltpu.VMEM((t, HG_D), F32), pltpu.VMEM((HG_D, HG_D), F32)],
        compiler_params=_cparams(("arbitrary",) * 4),
        name="hgrn_scan",
    )(u, u, u, u, lbp, onorm, msum, mask)


def _gla_kernel(q_ref, k_ref, v_ref, gate_ref, r_ref, w2_ref, gb_ref, on_ref, msum_ref, mask_ref, o_ref,
                ofwd_ref, st_ref, *, nblk, tb, c, nlev):
    ph = pl.program_id(2)
    jb = pl.program_id(3)
    blk = jnp.where(ph == 0, jb, nblk - 1 - jb)
    nch = tb // c

    @pl.when(jb == 0)
    def _():
        st_ref[...] = jnp.zeros_like(st_ref)

    msum = msum_ref[0]
    onorm = on_ref[...]
    w2 = w2_ref[0, 0]
    gb = gb_ref[0, 0]

    for ci in range(nch):
        cidx = jnp.where(ph == 0, ci, nch - 1 - ci)
        off = pl.multiple_of(cidx * c, c)
        rows = pl.ds(off, c)
        q = q_ref[0, rows, :] * (GLA_DK ** -0.5)
        k = k_ref[0, rows, :]
        v = v_ref[0, rows, :]
        z = _dot(r_ref[0, rows, :].astype(BF16), w2) + gb
        g = _log_sigmoid(z) / GLA_GATE_NORM
        o, st_new = _chunk_core(q, k, v, g, msum, mask_ref, st_ref[...], c=c, nlev=nlev)
        st_ref[...] = st_new
        grow = pl.ds(pl.multiple_of(blk * tb + cidx * c, c), c)

        @pl.when(ph == 0)
        def _():
            ofwd_ref[grow, :] = o

        @pl.when(ph == 1)
        def _():
            o_ref[0, rows, :] = _scan_finish(ofwd_ref[grow, :] + o, onorm, gate_ref[0, rows, :])


def _gla_scan(u, r, w2p, gbias, onorm, b, t):
    tb, c = min(SCAN_BLOCK, t), CHUNK
    nblk = t // tb
    msum, mask, nlev = _scan_consts(c)
    nh = GLA_HEADS
    vblk0 = 2 * nh * GLA_DK // GLA_DV

    def blk_of(ph, jb):
        return jnp.where(ph == 0, jb, nblk - 1 - jb)

    return pl.pallas_call(
        functools.partial(_gla_kernel, nblk=nblk, tb=tb, c=c, nlev=nlev),
        grid=(b, nh, 2, nblk),
        in_specs=[
            pl.BlockSpec((1, tb, GLA_DK), lambda bi, h, ph, jb: (bi, blk_of(ph, jb), h)),
            pl.BlockSpec((1, tb, GLA_DK), lambda bi, h, ph, jb: (bi, blk_of(ph, jb), nh + h)),
            pl.BlockSpec((1, tb, GLA_DV), lambda bi, h, ph, jb: (bi, blk_of(ph, jb), vblk0 + h)),
            pl.BlockSpec((1, tb, GLA_DV),
                         lambda bi, h, ph, jb: (bi, jnp.where(ph == 0, nblk - 1, nblk - 1 - jb), vblk0 + nh + h)),
            pl.BlockSpec((1, tb, GLA_R_PAD), lambda bi, h, ph, jb: (bi, blk_of(ph, jb), 0)),
            pl.BlockSpec((1, 1, GLA_R_PAD, GLA_DK), lambda bi, h, ph, jb: (ph, h, 0, 0)),
            pl.BlockSpec((1, 1, 1, GLA_DK), lambda bi, h, ph, jb: (ph, h, 0, 0)),
            pl.BlockSpec((1, GLA_DV), lambda bi, h, ph, jb: (0, 0)),
            pl.BlockSpec((1,) + msum.shape[1:], lambda bi, h, ph, jb: (ph, 0, 0)),
            pl.BlockSpec((1,) + mask.shape[1:], lambda bi, h, ph, jb: (ph, 0, 0, 0)),
        ],
        out_specs=pl.BlockSpec(
            (1, tb, GLA_DV), lambda bi, h, ph, jb: (bi, jnp.where(ph == 0, nblk - 1, nblk - 1 - jb), h)),
        out_shape=jax.ShapeDtypeStruct((b, t, nh * GLA_DV), BF16),
        scratch_shapes=[pltpu.VMEM((t, GLA_DV), F32), pltpu.VMEM((GLA_DV, GLA_DK), F32)],
        compiler_params=_cparams(("arbitrary",) * 4),
        name="gla_scan",
    )(u, u, u, u, r, w2p, gbias, onorm, msum, mask)


def _rope128(x, cos, sin):
    lane = lax.broadcasted_iota(jnp.int32, x.shape, 1)
    half = MLA_ROPE // 2
    swapped = jnp.where(lane < half, pltpu.roll(x, LANE - half, 1), pltpu.roll(x, half, 1))
    return x * cos + swapped * sin


def _mla_prep_kernel(c_ref, cos_ref, sin_ref, qag_ref, kvag_ref, wq_ref, wk_ref, wv_ref, gq_ref, gk_ref,
                     q_ref, k_ref, v_ref):
    c = c_ref[...]
    cq = c[:, :Q_LORA]
    ckv = c[:, Q_LORA:Q_LORA + KV_LORA]
    kpe = c[:, Q_LORA + KV_LORA:]
    cqn = (cq * lax.rsqrt(jnp.mean(cq * cq, axis=-1, keepdims=True) + EPS) * qag_ref[...]).astype(BF16)
    ckvn = (ckv * lax.rsqrt(jnp.mean(ckv * ckv, axis=-1, keepdims=True) + EPS) * kvag_ref[...]).astype(BF16)
    cos = cos_ref[...]
    sin = sin_ref[...]
    gq = gq_ref[...]
    gk = gk_ref[...]
    q_raw = _dot(cqn, wq_ref[...])
    kn_raw = _dot(ckvn, wk_ref[...])
    v_ref[...] = _dot(ckvn, wv_ref[...]).astype(BF16)
    kpe_ss = jnp.sum(kpe * kpe, axis=-1, keepdims=True)
    kpe_rot = _rope128(kpe * gk[:, LANE:], cos, sin)
    for h in range(MLA_HEADS):
        lo = h * MLA_QK_PAD
        qn = q_raw[:, lo:lo + LANE]
        qr = q_raw[:, lo + LANE:lo + 2 * LANE]
        ss = jnp.sum(qn * qn, axis=-1, keepdims=True) + jnp.sum(qr * qr, axis=-1, keepdims=True)
        rinv = lax.rsqrt(ss * (1.0 / MLA_QK) + EPS)
        q_ref[:, lo:lo + LANE] = (qn * rinv * gq[:, :LANE]).astype(BF16)
        q_ref[:, lo + LANE:lo + 2 * LANE] = _rope128(qr * rinv * gq[:, LANE:], cos, sin).astype(BF16)
        kn = kn_raw[:, h * LANE:(h + 1) * LANE]
        ssk = jnp.sum(kn * kn, axis=-1, keepdims=True) + kpe_ss
        rinvk = lax.rsqrt(ssk * (1.0 / MLA_QK) + EPS)
        k_ref[:, lo:lo + LANE] = (kn * rinvk * gk[:, :LANE]).astype(BF16)
        k_ref[:, lo + LANE:lo + 2 * LANE] = (kpe_rot * rinvk).astype(BF16)


def _mla_prep(c, cos, sin, qag, kvag, wq, wk, wv, gq, gk, t):
    n = c.shape[0]
    tm = min(256, t)
    tps = t // tm
    full = lambda a: pl.BlockSpec(a.shape, lambda i: (0,) * a.ndim)
    return pl.pallas_call(
        _mla_prep_kernel,
        grid=(n // tm,),
        in_specs=[
            pl.BlockSpec((tm, MLA_C_PAD), lambda i: (i, 0)),
            pl.BlockSpec((tm, LANE), lambda i: (i % tps, 0)),
            pl.BlockSpec((tm, LANE), lambda i: (i % tps, 0)),
            full(qag), full(kvag), full(wq), full(wk), full(wv), full(gq), full(gk),
        ],
        out_specs=[
            pl.BlockSpec((tm, MLA_HEADS * MLA_QK_PAD), lambda i: (i, 0)),
            pl.BlockSpec((tm, MLA_HEADS * MLA_QK_PAD), lambda i: (i, 0)),
            pl.BlockSpec((tm, MLA_HEADS * MLA_V), lambda i: (i, 0)),
        ],
        out_shape=[
            jax.ShapeDtypeStruct((n, MLA_HEADS * MLA_QK_PAD), BF16),
            jax.ShapeDtypeStruct((n, MLA_HEADS * MLA_QK_PAD), BF16),
            jax.ShapeDtypeStruct((n, MLA_HEADS * MLA_V), BF16),
        ],
        compiler_params=_cparams(("arbitrary",)),
        name="mla_prep",
    )(c, cos, sin, qag, kvag, wq, wk, wv, gq, gk)


def _attn_kernel(q_ref, k_ref, v_ref, o_ref, *, tk, nk, scale):
    q = q_ref[0]
    tq = q.shape[0]

    def body(ci, carry):
        m, l, acc = carry
        rows = pl.ds(pl.multiple_of(ci * tk, tk), tk)
        s = _dot_nt(q, k_ref[0, rows, :]) * scale
        m_new = jnp.maximum(m, jnp.max(s, axis=-1, keepdims=True))
        alpha = jnp.exp(m - m_new)
        p = jnp.exp(s - m_new)
        l = alpha * l + jnp.sum(p, axis=-1, keepdims=True)
        acc = alpha * acc + _dot(p.astype(BF16), v_ref[0, rows, :])
        return m_new, l, acc

    m0 = jnp.full((tq, 1), -jnp.inf, F32)
    l0 = jnp.zeros((tq, 1), F32)
    acc0 = jnp.zeros((tq, MLA_V), F32)
    _, l, acc = lax.fori_loop(0, nk, body, (m0, l0, acc0))
    o_ref[0] = (acc / l).astype(BF16)


def _attention(q, k, v, b, t):
    tq, tk = min(ATTN_TQ, t), min(ATTN_TK, t)
    return pl.pallas_call(
        functools.partial(_attn_kernel, tk=tk, nk=t // tk, scale=MLA_QK ** -0.5),
        grid=(b, MLA_HEADS, t // tq),
        in_specs=[
            pl.BlockSpec((1, tq, MLA_QK_PAD), lambda bi, h, i: (bi, i, h)),
            pl.BlockSpec((1, t, MLA_QK_PAD), lambda bi, h, i: (bi, 0, h)),
            pl.BlockSpec((1, t, MLA_V), lambda bi, h, i: (bi, 0, h)),
        ],
        out_specs=pl.BlockSpec((1, tq, MLA_V), lambda bi, h, i: (bi, i, h)),
        out_shape=jax.ShapeDtypeStruct((b, t, MLA_HEADS * MLA_V), BF16),
        compiler_params=_cparams(("arbitrary",) * 3),
        name="mla_attention",
    )(q, k, v)


def _rope_tables(t):
    half = MLA_ROPE // 2
    inv_freq = ROPE_THETA ** (-jnp.arange(half, dtype=F32) / half)
    ang = jnp.arange(t, dtype=jnp.int32).astype(F32)[:, None] * inv_freq[None, :]
    cos, sin = jnp.cos(ang), jnp.sin(ang)
    zeros = jnp.zeros((t, LANE - MLA_ROPE), F32)
    return (jnp.concatenate([cos, cos, zeros], axis=-1), jnp.concatenate([-sin, sin, zeros], axis=-1))


def _pad_cols(a, width):
    return jnp.pad(a, [(0, 0)] * (a.ndim - 1) + [(0, width - a.shape[-1])])


def _prep_even(e, w):
    ev_in = w['ev_w_in'][e]
    w_hg = ev_in[:, :5 * HG_WIDTH].astype(BF16)
    w_mla = _pad_cols(ev_in[:, 5 * HG_WIDTH:], MLA_C_PAD).astype(BF16)
    uq = w['mla_w_uq'][e].reshape(Q_LORA, MLA_HEADS, MLA_QK)
    wq = _pad_cols(uq, MLA_QK_PAD).reshape(Q_LORA, MLA_HEADS * MLA_QK_PAD).astype(BF16)
    ukv = w['mla_w_ukv'][e].reshape(KV_LORA, MLA_HEADS, MLA_NOPE + MLA_V)
    wk = ukv[:, :, :MLA_NOPE].reshape(KV_LORA, MLA_HEADS * MLA_NOPE).astype(BF16)
    wv = ukv[:, :, MLA_NOPE:].reshape(KV_LORA, MLA_HEADS * MLA_V).astype(BF16)
    p = jax.nn.softmax(w['hgrn_lb'].astype(F32), axis=1)
    lb = jnp.cumsum(p, axis=1)
    lb = (lb - lb[:, :1])[:, e].reshape(2, HG_HEADS, 1, HG_D)
    lbp = jnp.concatenate([jnp.log(lb), jnp.log1p(-lb), 1.0 - lb, jnp.zeros((2, HG_HEADS, 5, HG_D), F32)], axis=2)
    w_out = w['ev_w_out'][e].astype(BF16)
    return dict(
        w_hg=w_hg, w_mla=w_mla, wq=wq, wk=wk, wv=wv, lbp=lbp,
        hg_onorm=w['hgrn_onorm_g'][e].reshape(1, HG_D),
        qag=w['mla_qa_norm_g'][e].reshape(1, Q_LORA), kvag=w['mla_kva_norm_g'][e].reshape(1, KV_LORA),
        gq=_pad_cols(w['mla_qn_g'][e].reshape(1, MLA_QK), MLA_QK_PAD),
        gk=_pad_cols(w['mla_kn_g'][e].reshape(1, MLA_QK), MLA_QK_PAD),
        w_out_hg=w_out[:HG_WIDTH], w_out_mla=w_out[HG_WIDTH:],
    )


def _prep_odd(e, w):
    od_in = w['od_w_in'][e]
    w_main = od_in[:, :GLA_MAIN].astype(BF16)
    w_r = _pad_cols(od_in[:, GLA_MAIN:], GLA_R_PAD).astype(BF16)
    w2 = w['gla_gk_w2'][e].reshape(2, GLA_RANK, GLA_HEADS, GLA_DK).transpose(0, 2, 1, 3)
    w2p = jnp.zeros((2, GLA_HEADS, GLA_R_PAD, GLA_DK), F32)
    w2p = w2p.at[0, :, :GLA_RANK].set(w2[0]).at[1, :, GLA_RANK:2 * GLA_RANK].set(w2[1]).astype(BF16)
    gbias = w['gla_gk_b'][e].reshape(2, GLA_HEADS, 1, GLA_DK)
    return dict(w_main=w_main, w_r=w_r, w2p=w2p, gbias=gbias,
                onorm=w['gla_onorm_g'][e].reshape(1, GLA_DV), w_out=w['od_w_out'][e].astype(BF16))


def _trunk(x3, boff, nb_total, table, norm_g, w13, w2, evens, odds):
    b, t, _ = x3.shape
    n = b * t
    x = x3.reshape(n, D_MODEL)
    mod = _Mod(table, norm_g, nb_total, boff, t // TOKEN_TILE)
    cos, sin = _rope_tables(t)
    for layer in range(DEPTH):
        x = _ffn(x, mod, layer, 0, 0, w13, w2)
        e = layer // 2
        if layer % 2 == 0:
            p = evens[e]
            u_hg, c_mla = _inproj(x, mod, layer, p['w_hg'], p['w_mla'], 1024)
            o_hg = _hgrn_scan(u_hg.reshape(b, t, 5 * HG_WIDTH), p['lbp'], p['hg_onorm'], b, t)
            q, k, v = _mla_prep(c_mla, cos, sin, p['qag'], p['kvag'], p['wq'], p['wk'], p['wv'], p['gq'], p['gk'], t)
            o_mla = _attention(q.reshape(b, t, -1), k.reshape(b, t, -1), v.reshape(b, t, -1), b, t)
            x = _outproj(x, mod, layer, [o_hg.reshape(n, HG_WIDTH), o_mla.reshape(n, MLA_HEADS * MLA_V)],
                         [p['w_out_hg'], p['w_out_mla']])
        else:
            p = odds[e]
            u, r = _inproj(x, mod, layer, p['w_main'], p['w_r'], 1024)
            o = _gla_scan(u.reshape(b, t, GLA_MAIN), r.reshape(b, t, GLA_R_PAD), p['w2p'], p['gbias'], p['onorm'], b, t)
            x = _outproj(x, mod, layer, [o.reshape(n, GLA_HEADS * GLA_DV)], [p['w_out']])
        x = _ffn(x, mod, layer, 2, 1, w13, w2)
    return x.reshape(b, t, D_MODEL)


def kernel(x_prompt, x_sample, c_prompt, c_sample, ada_w, ada_b, norm_g, ffn_w13, ffn_w2, ev_w_in, ev_w_out, hgrn_lb, hgrn_onorm_g, mla_qa_norm_g, mla_w_uq, mla_kva_norm_g, mla_w_ukv, mla_qn_g, mla_kn_g, od_w_in, od_w_out, gla_gk_w2, gla_gk_b, gla_onorm_g):
    w = dict(ev_w_in=ev_w_in, ev_w_out=ev_w_out, hgrn_lb=hgrn_lb, hgrn_onorm_g=hgrn_onorm_g,
             mla_qa_norm_g=mla_qa_norm_g, mla_w_uq=mla_w_uq, mla_kva_norm_g=mla_kva_norm_g,
             mla_w_ukv=mla_w_ukv, mla_qn_g=mla_qn_g, mla_kn_g=mla_kn_g, od_w_in=od_w_in,
             od_w_out=od_w_out, gla_gk_w2=gla_gk_w2, gla_gk_b=gla_gk_b, gla_onorm_g=gla_onorm_g)
    bp, bs = x_prompt.shape[0], x_sample.shape[0]
    nb_total = bp + bs
    c_all = jnp.concatenate([c_prompt, c_sample], axis=0)
    mod = _ada_mod(c_all, ada_w, ada_b)
    table = mod.reshape(DEPTH * nb_total * 3 * N_SUB, 1, D_MODEL)
    ng = norm_g.reshape(DEPTH * N_SUB, 1, D_MODEL)
    w13 = ffn_w13.astype(BF16)
    w2 = ffn_w2.astype(BF16)
    evens = [_prep_even(e, w) for e in range((DEPTH + 1) // 2)]
    odds = [_prep_odd(e, w) for e in range(DEPTH // 2)]
    y_prompt = _trunk(x_prompt, 0, nb_total, table, ng, w13, w2, evens, odds)
    y_sample = _trunk(x_sample, bp, nb_total, table, ng, w13, w2, evens, odds)
    return (y_prompt, y_sample)
```

```python
import functools
import math

import numpy as np
import jax
import jax.numpy as jnp
from jax import lax
from jax.experimental import pallas as pl
from jax.experimental.pallas import tpu as pltpu

F32 = jnp.float32
BF16 = jnp.bfloat16

D_MODEL = 2048
DEPTH = 4
N_SUB = 3
EPS = 1e-6
FFN_DIM = 5632

HG_HEADS = 8
HG_D = 128
HG_WIDTH = HG_HEADS * HG_D

MLA_HEADS = 8
MLA_NOPE = 128
MLA_ROPE = 64
MLA_QK = MLA_NOPE + MLA_ROPE
MLA_V = 128
MLA_QK_PAD = 256
Q_LORA = 512
KV_LORA = 512
MLA_C_PAD = 1152
ROPE_THETA = 10000.0

GLA_HEADS = 4
GLA_DK = 256
GLA_DV = 512
GLA_RANK = 16
GLA_GATE_NORM = 16.0
GLA_MAIN = 2 * GLA_HEADS * GLA_DK + 2 * GLA_HEADS * GLA_DV
GLA_R_PAD = 128

CHUNK = 64
LANE = 128
VMEM_LIMIT = 56 * 1024 * 1024

TOKEN_TILE = 512
FFN_TILE = 512
SCAN_BLOCK = 512
ATTN_TQ = 512
ATTN_SPLIT = 2
ATTN_TK = 512
ATTN_Q_SCALE = MLA_QK ** -0.5 * math.log2(math.e)


def _cparams(sem):
    return pltpu.CompilerParams(dimension_semantics=sem, vmem_limit_bytes=VMEM_LIMIT)


def _silu(x):
    return x * jax.nn.sigmoid(x)


def _log_sigmoid(z):
    return jnp.minimum(z, 0.0) - jnp.log1p(jnp.exp(-jnp.abs(z)))


def _adaln(x, g, scale, shift):
    ms = jnp.mean(x * x, axis=-1, keepdims=True)
    y = x * lax.rsqrt(ms + EPS) * g
    return y * (1.0 + scale) + shift


def _dot(a, b):
    return jnp.dot(a, b, preferred_element_type=F32)


def _dot_nt(a, b):
    return lax.dot_general(a, b, (((1,), (1,)), ((), ())), preferred_element_type=F32)


def _dot_tn(a, b):
    return lax.dot_general(a, b, (((0,), (0,)), ((), ())), preferred_element_type=F32)


def _ada_kernel(c_ref, w_ref, b_ref, o_ref):
    cond = _silu(c_ref[...]).astype(BF16)
    o_ref[0] = _dot(cond, w_ref[0].astype(BF16)) + b_ref[0]


def _ada_mod(c_all, ada_w, ada_b):
    nb = c_all.shape[0]
    n_out = ada_w.shape[-1]
    tn = 1024
    return pl.pallas_call(
        _ada_kernel,
        grid=(DEPTH, n_out // tn),
        in_specs=[
            pl.BlockSpec((nb, D_MODEL), lambda l, j: (0, 0)),
            pl.BlockSpec((1, D_MODEL, tn), lambda l, j: (l, 0, j)),
            pl.BlockSpec((1, 1, tn), lambda l, j: (l, 0, j)),
        ],
        out_specs=pl.BlockSpec((1, nb, tn), lambda l, j: (l, 0, j)),
        out_shape=jax.ShapeDtypeStruct((DEPTH, nb, n_out), F32),
        compiler_params=_cparams(("arbitrary", "arbitrary")),
        name="ada_mod",
    )(c_all, ada_w, ada_b.reshape(DEPTH, 1, n_out))


class _Mod:
    def __init__(self, table, norm_g, nb_total, boff, tiles_per_seq):
        self.table = table
        self.norm_g = norm_g
        self.nb_total = nb_total
        self.boff = boff
        self.tps = tiles_per_seq

    def spec(self, layer, sub, kind):
        nb, boff, tps = self.nb_total, self.boff, self.tps
        return pl.BlockSpec(
            (1, 1, D_MODEL),
            lambda i, *_: (((layer * nb + boff + i // tps) * 9 + sub * 3 + kind), 0, 0))

    def norm_spec(self, layer, sub):
        return pl.BlockSpec((1, 1, D_MODEL), lambda i, *_: (layer * N_SUB + sub, 0, 0))


def _ffn_kernel(x_ref, sh_ref, sc_ref, gt_ref, ng_ref, w1_ref, w3_ref, w2_ref, o_ref, h_ref, acc_ref, *, nj):
    j = pl.program_id(1)

    @pl.when(j == 0)
    def _():
        h_ref[...] = _adaln(x_ref[...], ng_ref[0], sc_ref[0], sh_ref[0]).astype(BF16)
        acc_ref[...] = jnp.zeros_like(acc_ref)

    h = h_ref[...]
    a = _dot(h, w1_ref[...])
    u = _dot(h, w3_ref[...])
    act = (_silu(a) * u).astype(BF16)
    acc_ref[...] += _dot(act, w2_ref[...])

    @pl.when(j == nj - 1)
    def _():
        o_ref[...] = x_ref[...] + (0.5 * gt_ref[0]) * acc_ref[...]


def _ffn(x, mod, layer, sub, which, w13, w2):
    n = x.shape[0]
    tm, tf = TOKEN_TILE, FFN_TILE
    nj = FFN_DIM // tf
    return pl.pallas_call(
        functools.partial(_ffn_kernel, nj=nj),
        grid=(n // tm, nj),
        in_specs=[
            pl.BlockSpec((tm, D_MODEL), lambda i, j: (i, 0)),
            mod.spec(layer, sub, 0), mod.spec(layer, sub, 1), mod.spec(layer, sub, 2),
            mod.norm_spec(layer, sub),
            pl.BlockSpec((None, None, D_MODEL, tf), lambda i, j: (layer, which, 0, j)),
            pl.BlockSpec((None, None, D_MODEL, tf), lambda i, j: (layer, which, 0, j + nj)),
            pl.BlockSpec((None, None, tf, D_MODEL), lambda i, j: (layer, which, j, 0)),
        ],
        out_specs=pl.BlockSpec((tm, D_MODEL), lambda i, j: (i, 0)),
        out_shape=jax.ShapeDtypeStruct((n, D_MODEL), F32),
        scratch_shapes=[pltpu.VMEM((tm, D_MODEL), BF16), pltpu.VMEM((tm, D_MODEL), F32)],
        compiler_params=_cparams(("arbitrary", "arbitrary")),
        name="ffn",
    )(x, mod.table, mod.table, mod.table, mod.norm_g, w13, w13, w2)


def _inproj_kernel(x_ref, sh_ref, sc_ref, ng_ref, wm_ref, ws_ref, om_ref, os_ref, h_ref):
    @pl.when(pl.program_id(1) == 0)
    def _():
        h = _adaln(x_ref[...], ng_ref[0], sc_ref[0], sh_ref[0]).astype(BF16)
        h_ref[...] = h
        os_ref[...] = _dot(h, ws_ref[...])

    om_ref[...] = _dot(h_ref[...], wm_ref[...])


def _inproj(x, mod, layer, w_main, w_small, tn):
    n = x.shape[0]
    tm = TOKEN_TILE
    n_main, n_small = w_main.shape[1], w_small.shape[1]
    return pl.pallas_call(
        _inproj_kernel,
        grid=(n // tm, n_main // tn),
        in_specs=[
            pl.BlockSpec((tm, D_MODEL), lambda i, j: (i, 0)),
            mod.spec(layer, 1, 0), mod.spec(layer, 1, 1),
            mod.norm_spec(layer, 1),
            pl.BlockSpec((D_MODEL, tn), lambda i, j: (0, j)),
            pl.BlockSpec((D_MODEL, n_small), lambda i, j: (0, 0)),
        ],
        out_specs=[
            pl.BlockSpec((tm, tn), lambda i, j: (i, j)),
            pl.BlockSpec((tm, n_small), lambda i, j: (i, 0)),
        ],
        out_shape=[jax.ShapeDtypeStruct((n, n_main), F32), jax.ShapeDtypeStruct((n, n_small), F32)],
        scratch_shapes=[pltpu.VMEM((tm, D_MODEL), BF16)],
        compiler_params=_cparams(("arbitrary", "arbitrary")),
        name="inproj",
    )(x, mod.table, mod.table, mod.norm_g, w_main, w_small)


def _outproj_kernel(x_ref, gt_ref, *refs):
    o_ref = refs[-1]
    npair = (len(refs) - 1) // 2
    y = _dot(refs[0][...], refs[npair][...])
    for p in range(1, npair):
        y = y + _dot(refs[p][...], refs[npair + p][...])
    o_ref[...] = x_ref[...] + gt_ref[0] * y


def _outproj(x, mod, layer, mixes, ws):
    n = x.shape[0]
    tm = TOKEN_TILE
    in_specs = [pl.BlockSpec((tm, D_MODEL), lambda i: (i, 0)), mod.spec(layer, 1, 2)]
    in_specs += [pl.BlockSpec((tm, m.shape[1]), lambda i: (i, 0)) for m in mixes]
    in_specs += [pl.BlockSpec(w.shape, lambda i: (0, 0)) for w in ws]
    return pl.pallas_call(
        _outproj_kernel,
        grid=(n // tm,),
        in_specs=in_specs,
        out_specs=pl.BlockSpec((tm, D_MODEL), lambda i: (i, 0)),
        out_shape=jax.ShapeDtypeStruct((n, D_MODEL), F32),
        compiler_params=_cparams(("arbitrary",)),
        name="outproj",
    )(x, mod.table, *mixes, *ws)


def _scan_consts(c):
    nlev = int(math.log2(c))
    t = np.arange(c)
    row, col = t[:, None], t[None, :]
    blocks = [col <= row]
    masks = [np.eye(c, dtype=bool)]
    for lev in range(nlev):
        s = 1 << lev
        blk = t // s
        odd = (blk % 2) == 1
        bstart = (blk * s)[:, None]
        bend = bstart + s - 1
        as_query = (col >= bstart) & (col <= row)
        as_key = (col > row) & (col <= bend)
        blocks.append(np.where(odd[:, None], as_query, as_key))
        masks.append(odd[:, None] & (~odd[None, :]) & ((row // (2 * s)) == (col // (2 * s))))
    ones = np.ones((8, c), dtype=bool)
    m_f = np.concatenate(blocks + [ones], axis=0)
    m_b = np.concatenate([b[::-1, ::-1] for b in blocks] + [ones], axis=0)
    msum = jnp.asarray(np.stack([m_f, m_b]).astype(np.float32), dtype=BF16)
    mask = jnp.asarray(np.stack([np.stack(masks), np.stack([m[::-1, ::-1] for m in masks])]).astype(np.float32))
    return msum, mask, nlev


def _split2(g):
    hi = g.astype(BF16)
    lo = (g - hi.astype(F32)).astype(BF16)
    return hi, lo


def _scan_block(ph, gate_input, load_chunk, msum_ref, mask_ref, st_ref, oblk_ref, *, nch, c, nlev):
    msum = msum_ref[0]
    masks = [mask_ref[0, i] > 0.5 for i in range(nlev + 1)]
    order = []
    for ci in range(nch):
        cidx = jnp.where(ph == 0, ci, nch - 1 - ci)
        order.append(pl.ds(pl.multiple_of(cidx * c, c), c))

    gate_in = [gate_input(rows) for rows in order]
    stage1 = []
    for rows, z in zip(order, gate_in):
        q, k, v, g = load_chunk(rows, z)
        dk = q.shape[-1]
        hi, lo = _split2(g)
        x2 = _dot(msum, jnp.concatenate([hi, lo], axis=-1))
        stage1.append((q, k, v.astype(BF16), x2[:, :dk] + x2[:, dk:]))

    stage2 = []
    for q, k, vb, x in stage1:
        e = jnp.exp(x)
        tot = x[(1 + nlev) * c:(1 + nlev) * c + 1]
        dtot = e[(1 + nlev) * c:(1 + nlev) * c + 1]
        qe = (q * e[0:c]).astype(BF16)
        ke = (k * jnp.exp(tot - x[0:c])).astype(BF16)
        att = jnp.where(masks[0], _dot_nt(q.astype(BF16), k.astype(BF16)), 0.0)
        for lev in range(nlev):
            es = e[(1 + lev) * c:(2 + lev) * c]
            att = jnp.where(masks[1 + lev], _dot_nt((q * es).astype(BF16), (k * es).astype(BF16)), att)
        stage2.append((qe, dtot, _dot(att.astype(BF16), vb), _dot_tn(vb, ke)))

    st = st_ref[...]
    for rows, (qe, dtot, o_intra, inc) in zip(order, stage2):
        oblk_ref[rows, :] = o_intra + _dot_nt(qe, st.astype(BF16))
        st = st * dtot + inc
    st_ref[...] = st


def _scan_emit(ph, blk, tb, oblk_ref, ofwd_ref, o_ref, on_ref, gate_ref):
    grows = pl.ds(pl.multiple_of(blk * tb, tb), tb)

    @pl.when(ph == 0)
    def _():
        ofwd_ref[grows, :] = oblk_ref[...]

    @pl.when(ph == 1)
    def _():
        o_sum = ofwd_ref[grows, :] + oblk_ref[...]
        ms = jnp.mean(o_sum * o_sum, axis=-1, keepdims=True)
        o_ref[0] = (o_sum * lax.rsqrt(ms + EPS) * on_ref[...] * _silu(gate_ref[0])).astype(BF16)


def _hgrn_kernel(q_ref, z_ref, v_ref, gate_ref, lb_ref, on_ref, msum_ref, mask_ref, o_ref, ofwd_ref, st_ref,
                 oblk_ref, *, nblk, tb, c, nlev):
    ph = pl.program_id(2)
    jb = pl.program_id(3)
    blk = jnp.where(ph == 0, jb, nblk - 1 - jb)

    @pl.when(jb == 0)
    def _():
        st_ref[...] = jnp.zeros_like(st_ref)

    log_lb = lb_ref[0, 0, 0:1, :]
    log_1mlb = lb_ref[0, 0, 1:2, :]
    one_mlb = lb_ref[0, 0, 2:3, :]

    def gate_input(rows):
        return z_ref[0, rows, :]

    def load_chunk(rows, z):
        q = _silu(q_ref[0, rows, :])
        a = log_lb
        b = log_1mlb + _log_sigmoid(z)
        g = jnp.maximum(a, b) + jnp.log1p(jnp.exp(-jnp.abs(a - b)))
        k = one_mlb * jax.nn.sigmoid(-z)
        return q, k, v_ref[0, rows, :], g

    _scan_block(ph, gate_input, load_chunk, msum_ref, mask_ref, st_ref, oblk_ref, nch=tb // c, c=c, nlev=nlev)
    _scan_emit(ph, blk, tb, oblk_ref, ofwd_ref, o_ref, on_ref, gate_ref)


def _hgrn_scan(u, lbp, onorm, b, t):
    tb, c = min(SCAN_BLOCK, t), CHUNK
    nblk = t // tb
    msum, mask, nlev = _scan_consts(c)
    nh = HG_HEADS

    def blk_of(ph, jb):
        return jnp.where(ph == 0, jb, nblk - 1 - jb)

    return pl.pallas_call(
        functools.partial(_hgrn_kernel, nblk=nblk, tb=tb, c=c, nlev=nlev),
        grid=(b, nh, 2, nblk),
        in_specs=[
            pl.BlockSpec((1, tb, HG_D), lambda bi, h, ph, jb: (bi, blk_of(ph, jb), h)),
            pl.BlockSpec((1, tb, HG_D), lambda bi, h, ph, jb: (bi, blk_of(ph, jb), nh + nh * ph + h)),
            pl.BlockSpec((1, tb, HG_D), lambda bi, h, ph, jb: (bi, blk_of(ph, jb), 3 * nh + h)),
            pl.BlockSpec((1, tb, HG_D),
                         lambda bi, h, ph, jb: (bi, jnp.where(ph == 0, nblk - 1, nblk - 1 - jb), 4 * nh + h)),
            pl.BlockSpec((1, 1, 8, HG_D), lambda bi, h, ph, jb: (ph, h, 0, 0)),
            pl.BlockSpec((1, HG_D), lambda bi, h, ph, jb: (0, 0)),
            pl.BlockSpec((1,) + msum.shape[1:], lambda bi, h, ph, jb: (ph, 0, 0)),
            pl.BlockSpec((1,) + mask.shape[1:], lambda bi, h, ph, jb: (ph, 0, 0, 0)),
        ],
        out_specs=pl.BlockSpec(
            (1, tb, HG_D), lambda bi, h, ph, jb: (bi, jnp.where(ph == 0, nblk - 1, nblk - 1 - jb), h)),
        out_shape=jax.ShapeDtypeStruct((b, t, HG_WIDTH), BF16),
        scratch_shapes=[pltpu.VMEM((t, HG_D), F32), pltpu.VMEM((HG_D, HG_D), F32), pltpu.VMEM((tb, HG_D), F32)],
        compiler_params=_cparams(("arbitrary",) * 4),
        name="hgrn_scan",
    )(u, u, u, u, lbp, onorm, msum, mask)


def _gla_kernel(q_ref, k_ref, v_ref, gate_ref, r_ref, w2_ref, gb_ref, on_ref, msum_ref, mask_ref, o_ref,
                ofwd_ref, st_ref, oblk_ref, *, nblk, tb, c, nlev):
    ph = pl.program_id(2)
    jb = pl.program_id(3)
    blk = jnp.where(ph == 0, jb, nblk - 1 - jb)

    @pl.when(jb == 0)
    def _():
        st_ref[...] = jnp.zeros_like(st_ref)

    w2 = w2_ref[0, 0]
    gb = gb_ref[0, 0]

    def gate_input(rows):
        return _dot(r_ref[0, rows, :].astype(BF16), w2) + gb

    def load_chunk(rows, z):
        q = q_ref[0, rows, :] * (GLA_DK ** -0.5)
        g = _log_sigmoid(z) * (1.0 / GLA_GATE_NORM)
        return q, k_ref[0, rows, :], v_ref[0, rows, :], g

    _scan_block(ph, gate_input, load_chunk, msum_ref, mask_ref, st_ref, oblk_ref, nch=tb // c, c=c, nlev=nlev)
    _scan_emit(ph, blk, tb, oblk_ref, ofwd_ref, o_ref, on_ref, gate_ref)


def _gla_scan(u, r, w2p, gbias, onorm, b, t):
    tb, c = min(SCAN_BLOCK, t), CHUNK
    nblk = t // tb
    msum, mask, nlev = _scan_consts(c)
    nh = GLA_HEADS
    vblk0 = 2 * nh * GLA_DK // GLA_DV

    def blk_of(ph, jb):
        return jnp.where(ph == 0, jb, nblk - 1 - jb)

    return pl.pallas_call(
        functools.partial(_gla_kernel, nblk=nblk, tb=tb, c=c, nlev=nlev),
        grid=(b, nh, 2, nblk),
        in_specs=[
            pl.BlockSpec((1, tb, GLA_DK), lambda bi, h, ph, jb: (bi, blk_of(ph, jb), h)),
            pl.BlockSpec((1, tb, GLA_DK), lambda bi, h, ph, jb: (bi, blk_of(ph, jb), nh + h)),
            pl.BlockSpec((1, tb, GLA_DV), lambda bi, h, ph, jb: (bi, blk_of(ph, jb), vblk0 + h)),
            pl.BlockSpec((1, tb, GLA_DV),
                         lambda bi, h, ph, jb: (bi, jnp.where(ph == 0, nblk - 1, nblk - 1 - jb), vblk0 + nh + h)),
            pl.BlockSpec((1, tb, GLA_R_PAD), lambda bi, h, ph, jb: (bi, blk_of(ph, jb), 0)),
            pl.BlockSpec((1, 1, GLA_R_PAD, GLA_DK), lambda bi, h, ph, jb: (ph, h, 0, 0)),
            pl.BlockSpec((1, 1, 1, GLA_DK), lambda bi, h, ph, jb: (ph, h, 0, 0)),
            pl.BlockSpec((1, GLA_DV), lambda bi, h, ph, jb: (0, 0)),
            pl.BlockSpec((1,) + msum.shape[1:], lambda bi, h, ph, jb: (ph, 0, 0)),
            pl.BlockSpec((1,) + mask.shape[1:], lambda bi, h, ph, jb: (ph, 0, 0, 0)),
        ],
        out_specs=pl.BlockSpec(
            (1, tb, GLA_DV), lambda bi, h, ph, jb: (bi, jnp.where(ph == 0, nblk - 1, nblk - 1 - jb), h)),
        out_shape=jax.ShapeDtypeStruct((b, t, nh * GLA_DV), BF16),
        scratch_shapes=[pltpu.VMEM((t, GLA_DV), F32), pltpu.VMEM((GLA_DV, GLA_DK), F32),
                        pltpu.VMEM((tb, GLA_DV), F32)],
        compiler_params=_cparams(("arbitrary",) * 4),
        name="gla_scan",
    )(u, u, u, u, r, w2p, gbias, onorm, msum, mask)


def _rope128(x, cos, sin):
    lane = lax.broadcasted_iota(jnp.int32, x.shape, 1)
    half = MLA_ROPE // 2
    swapped = jnp.where(lane < half, pltpu.roll(x, LANE - half, 1), pltpu.roll(x, half, 1))
    return x * cos + swapped * sin


def _mla_prep_kernel(c_ref, cos_ref, sin_ref, qag_ref, kvag_ref, wq_ref, wk_ref, wv_ref, gq_ref, gk_ref,
                     q_ref, k_ref, v_ref):
    c = c_ref[...]
    cq = c[:, :Q_LORA]
    ckv = c[:, Q_LORA:Q_LORA + KV_LORA]
    kpe = c[:, Q_LORA + KV_LORA:]
    cqn = (cq * lax.rsqrt(jnp.mean(cq * cq, axis=-1, keepdims=True) + EPS) * qag_ref[...]).astype(BF16)
    ckvn = (ckv * lax.rsqrt(jnp.mean(ckv * ckv, axis=-1, keepdims=True) + EPS) * kvag_ref[...]).astype(BF16)
    cos = cos_ref[...]
    sin = sin_ref[...]
    gq = gq_ref[...]
    gk = gk_ref[...]
    q_raw = _dot(cqn, wq_ref[...])
    kn_raw = _dot(ckvn, wk_ref[...])
    v_ref[...] = _dot(ckvn, wv_ref[...]).astype(BF16)
    kpe_ss = jnp.sum(kpe * kpe, axis=-1, keepdims=True)
    kpe_rot = _rope128(kpe * gk[:, LANE:], cos, sin)
    for h in range(MLA_HEADS):
        lo = h * MLA_QK_PAD
        qn = q_raw[:, lo:lo + LANE]
        qr = q_raw[:, lo + LANE:lo + 2 * LANE]
        ss = jnp.sum(qn * qn, axis=-1, keepdims=True) + jnp.sum(qr * qr, axis=-1, keepdims=True)
        rinv = lax.rsqrt(ss * (1.0 / MLA_QK) + EPS) * ATTN_Q_SCALE
        q_ref[:, lo:lo + LANE] = (qn * rinv * gq[:, :LANE]).astype(BF16)
        q_ref[:, lo + LANE:lo + 2 * LANE] = _rope128(qr * rinv * gq[:, LANE:], cos, sin).astype(BF16)
        kn = kn_raw[:, h * LANE:(h + 1) * LANE]
        ssk = jnp.sum(kn * kn, axis=-1, keepdims=True) + kpe_ss
        rinvk = lax.rsqrt(ssk * (1.0 / MLA_QK) + EPS)
        k_ref[:, lo:lo + LANE] = (kn * rinvk * gk[:, :LANE]).astype(BF16)
        k_ref[:, lo + LANE:lo + 2 * LANE] = (kpe_rot * rinvk).astype(BF16)


def _mla_prep(c, cos, sin, qag, kvag, wq, wk, wv, gq, gk, t):
    n = c.shape[0]
    tm = min(256, t)
    tps = t // tm
    full = lambda a: pl.BlockSpec(a.shape, lambda i: (0,) * a.ndim)
    return pl.pallas_call(
        _mla_prep_kernel,
        grid=(n // tm,),
        in_specs=[
            pl.BlockSpec((tm, MLA_C_PAD), lambda i: (i, 0)),
            pl.BlockSpec((tm, LANE), lambda i: (i % tps, 0)),
            pl.BlockSpec((tm, LANE), lambda i: (i % tps, 0)),
            full(qag), full(kvag), full(wq), full(wk), full(wv), full(gq), full(gk),
        ],
        out_specs=[
            pl.BlockSpec((tm, MLA_HEADS * MLA_QK_PAD), lambda i: (i, 0)),
            pl.BlockSpec((tm, MLA_HEADS * MLA_QK_PAD), lambda i: (i, 0)),
            pl.BlockSpec((tm, MLA_HEADS * MLA_V), lambda i: (i, 0)),
        ],
        out_shape=[
            jax.ShapeDtypeStruct((n, MLA_HEADS * MLA_QK_PAD), BF16),
            jax.ShapeDtypeStruct((n, MLA_HEADS * MLA_QK_PAD), BF16),
            jax.ShapeDtypeStruct((n, MLA_HEADS * MLA_V), BF16),
        ],
        compiler_params=_cparams(("arbitrary",)),
        name="mla_prep",
    )(c, cos, sin, qag, kvag, wq, wk, wv, gq, gk)


def _attn_kernel(q_ref, k_ref, v_ref, o_ref, *, tk, nk, nsplit):
    tq = q_ref.shape[1]
    th = tq // nsplit
    qs = [q_ref[0, i * th:(i + 1) * th, :] for i in range(nsplit)]

    def scores(ci):
        kc = k_ref[0, ci * tk:(ci + 1) * tk, :]
        return [_dot_nt(q, kc) for q in qs]

    def update(ci, s_list, state):
        vc = v_ref[0, ci * tk:(ci + 1) * tk, :]
        out = []
        for s, (m, l, acc) in zip(s_list, state):
            m_new = jnp.maximum(m, jnp.max(s, axis=-1, keepdims=True))
            alpha = jnp.exp2(m - m_new)
            p = jnp.exp2(s - m_new)
            l = alpha * l + jnp.sum(p, axis=-1, keepdims=True)
            acc = alpha * acc + _dot(p.astype(BF16), vc)
            out.append((m_new, l, acc))
        return out

    state = [(jnp.full((th, 1), -jnp.inf, F32), jnp.zeros((th, 1), F32), jnp.zeros((th, MLA_V), F32))
             for _ in range(nsplit)]
    s_cur = scores(0)
    for ci in range(nk):
        s_next = scores(ci + 1) if ci + 1 < nk else None
        state = update(ci, s_cur, state)
        s_cur = s_next
    for i, (_, l, acc) in enumerate(state):
        o_ref[0, i * th:(i + 1) * th, :] = (acc / l).astype(BF16)


def _attention(q, k, v, b, t):
    tq, tk = min(ATTN_TQ, t), min(ATTN_TK, t)
    return pl.pallas_call(
        functools.partial(_attn_kernel, tk=tk, nk=t // tk, nsplit=ATTN_SPLIT),
        grid=(b, MLA_HEADS, t // tq),
        in_specs=[
            pl.BlockSpec((1, tq, MLA_QK_PAD), lambda bi, h, i: (bi, i, h)),
            pl.BlockSpec((1, t, MLA_QK_PAD), lambda bi, h, i: (bi, 0, h)),
            pl.BlockSpec((1, t, MLA_V), lambda bi, h, i: (bi, 0, h)),
        ],
        out_specs=pl.BlockSpec((1, tq, MLA_V), lambda bi, h, i: (bi, i, h)),
        out_shape=jax.ShapeDtypeStruct((b, t, MLA_HEADS * MLA_V), BF16),
        compiler_params=_cparams(("arbitrary",) * 3),
        name="mla_attention",
    )(q, k, v)


def _rope_tables(t):
    half = MLA_ROPE // 2
    inv_freq = ROPE_THETA ** (-jnp.arange(half, dtype=F32) / half)
    ang = jnp.arange(t, dtype=jnp.int32).astype(F32)[:, None] * inv_freq[None, :]
    cos, sin = jnp.cos(ang), jnp.sin(ang)
    zeros = jnp.zeros((t, LANE - MLA_ROPE), F32)
    return (jnp.concatenate([cos, cos, zeros], axis=-1), jnp.concatenate([-sin, sin, zeros], axis=-1))


def _pad_cols(a, width):
    return jnp.pad(a, [(0, 0)] * (a.ndim - 1) + [(0, width - a.shape[-1])])


def _prep_even(e, w):
    ev_in = w['ev_w_in'][e]
    w_hg = ev_in[:, :5 * HG_WIDTH].astype(BF16)
    w_mla = _pad_cols(ev_in[:, 5 * HG_WIDTH:], MLA_C_PAD).astype(BF16)
    uq = w['mla_w_uq'][e].reshape(Q_LORA, MLA_HEADS, MLA_QK)
    wq = _pad_cols(uq, MLA_QK_PAD).reshape(Q_LORA, MLA_HEADS * MLA_QK_PAD).astype(BF16)
    ukv = w['mla_w_ukv'][e].reshape(KV_LORA, MLA_HEADS, MLA_NOPE + MLA_V)
    wk = ukv[:, :, :MLA_NOPE].reshape(KV_LORA, MLA_HEADS * MLA_NOPE).astype(BF16)
    wv = ukv[:, :, MLA_NOPE:].reshape(KV_LORA, MLA_HEADS * MLA_V).astype(BF16)
    p = jax.nn.softmax(w['hgrn_lb'].astype(F32), axis=1)
    lb = jnp.cumsum(p, axis=1)
    lb = (lb - lb[:, :1])[:, e].reshape(2, HG_HEADS, 1, HG_D)
    lbp = jnp.concatenate([jnp.log(lb), jnp.log1p(-lb), 1.0 - lb, jnp.zeros((2, HG_HEADS, 5, HG_D), F32)], axis=2)
    w_out = w['ev_w_out'][e].astype(BF16)
    return dict(
        w_hg=w_hg, w_mla=w_mla, wq=wq, wk=wk, wv=wv, lbp=lbp,
        hg_onorm=w['hgrn_onorm_g'][e].reshape(1, HG_D),
        qag=w['mla_qa_norm_g'][e].reshape(1, Q_LORA), kvag=w['mla_kva_norm_g'][e].reshape(1, KV_LORA),
        gq=_pad_cols(w['mla_qn_g'][e].reshape(1, MLA_QK), MLA_QK_PAD),
        gk=_pad_cols(w['mla_kn_g'][e].reshape(1, MLA_QK), MLA_QK_PAD),
        w_out_hg=w_out[:HG_WIDTH], w_out_mla=w_out[HG_WIDTH:],
    )


def _prep_odd(e, w):
    od_in = w['od_w_in'][e]
    w_main = od_in[:, :GLA_MAIN].astype(BF16)
    w_r = _pad_cols(od_in[:, GLA_MAIN:], GLA_R_PAD).astype(BF16)
    w2 = w['gla_gk_w2'][e].reshape(2, GLA_RANK, GLA_HEADS, GLA_DK).transpose(0, 2, 1, 3)
    w2p = jnp.zeros((2, GLA_HEADS, GLA_R_PAD, GLA_DK), F32)
    w2p = w2p.at[0, :, :GLA_RANK].set(w2[0]).at[1, :, GLA_RANK:2 * GLA_RANK].set(w2[1]).astype(BF16)
    gbias = w['gla_gk_b'][e].reshape(2, GLA_HEADS, 1, GLA_DK)
    return dict(w_main=w_main, w_r=w_r, w2p=w2p, gbias=gbias,
                onorm=w['gla_onorm_g'][e].reshape(1, GLA_DV), w_out=w['od_w_out'][e].astype(BF16))


def _trunk(x3, boff, nb_total, table, norm_g, w13, w2, evens, odds):
    b, t, _ = x3.shape
    n = b * t
    x = x3.reshape(n, D_MODEL)
    mod = _Mod(table, norm_g, nb_total, boff, t // TOKEN_TILE)
    cos, sin = _rope_tables(t)
    for layer in range(DEPTH):
        x = _ffn(x, mod, layer, 0, 0, w13, w2)
        e = layer // 2
        if layer % 2 == 0:
            p = evens[e]
            u_hg, c_mla = _inproj(x, mod, layer, p['w_hg'], p['w_mla'], 1024)
            o_hg = _hgrn_scan(u_hg.reshape(b, t, 5 * HG_WIDTH), p['lbp'], p['hg_onorm'], b, t)
            q, k, v = _mla_prep(c_mla, cos, sin, p['qag'], p['kvag'], p['wq'], p['wk'], p['wv'], p['gq'], p['gk'], t)
            o_mla = _attention(q.reshape(b, t, -1), k.reshape(b, t, -1), v.reshape(b, t, -1), b, t)
            x = _outproj(x, mod, layer, [o_hg.reshape(n, HG_WIDTH), o_mla.reshape(n, MLA_HEADS * MLA_V)],
                         [p['w_out_hg'], p['w_out_mla']])
        else:
            p = odds[e]
            u, r = _inproj(x, mod, layer, p['w_main'], p['w_r'], 1024)
            o = _gla_scan(u.reshape(b, t, GLA_MAIN), r.reshape(b, t, GLA_R_PAD), p['w2p'], p['gbias'], p['onorm'], b, t)
            x = _outproj(x, mod, layer, [o.reshape(n, GLA_HEADS * GLA_DV)], [p['w_out']])
        x = _ffn(x, mod, layer, 2, 1, w13, w2)
    return x.reshape(b, t, D_MODEL)


def kernel(x_prompt, x_sample, c_prompt, c_sample, ada_w, ada_b, norm_g, ffn_w13, ffn_w2, ev_w_in, ev_w_out, hgrn_lb, hgrn_onorm_g, mla_qa_norm_g, mla_w_uq, mla_kva_norm_g, mla_w_ukv, mla_qn_g, mla_kn_g, od_w_in, od_w_out, gla_gk_w2, gla_gk_b, gla_onorm_g):
    w = dict(ev_w_in=ev_w_in, ev_w_out=ev_w_out, hgrn_lb=hgrn_lb, hgrn_onorm_g=hgrn_onorm_g,
             mla_qa_norm_g=mla_qa_norm_g, mla_w_uq=mla_w_uq, mla_kva_norm_g=mla_kva_norm_g,
             mla_w_ukv=mla_w_ukv, mla_qn_g=mla_qn_g, mla_kn_g=mla_kn_g, od_w_in=od_w_in,
             od_w_out=od_w_out, gla_gk_w2=gla_gk_w2, gla_gk_b=gla_gk_b, gla_onorm_g=gla_onorm_g)
    bp, bs = x_prompt.shape[0], x_sample.shape[0]
    nb_total = bp + bs
    c_all = jnp.concatenate([c_prompt, c_sample], axis=0)
    mod = _ada_mod(c_all, ada_w, ada_b)
    table = mod.reshape(DEPTH * nb_total * 3 * N_SUB, 1, D_MODEL)
    ng = norm_g.reshape(DEPTH * N_SUB, 1, D_MODEL)
    w13 = ffn_w13.astype(BF16)
    w2 = ffn_w2.astype(BF16)
    evens = [_prep_even(e, w) for e in range((DEPTH + 1) // 2)]
    odds = [_prep_odd(e, w) for e in range(DEPTH // 2)]
    y_prompt = _trunk(x_prompt, 0, nb_total, table, ng, w13, w2, evens, odds)
    y_sample = _trunk(x_sample, bp, nb_total, table, ng, w13, w2, evens, odds)
    return (y_prompt, y_sample)
```

```python
import functools
import math

import numpy as np
import jax
import jax.numpy as jnp
from jax import lax
from jax.experimental import pallas as pl
from jax.experimental.pallas import tpu as pltpu

F32 = jnp.float32
BF16 = jnp.bfloat16

D_MODEL = 2048
DEPTH = 4
N_SUB = 3
EPS = 1e-6
FFN_DIM = 5632

HG_HEADS = 8
HG_D = 128
HG_WIDTH = HG_HEADS * HG_D

MLA_HEADS = 8
MLA_NOPE = 128
MLA_ROPE = 64
MLA_QK = MLA_NOPE + MLA_ROPE
MLA_V = 128
MLA_QK_PAD = 256
Q_LORA = 512
KV_LORA = 512
MLA_C_PAD = 1152
ROPE_THETA = 10000.0

GLA_HEADS = 4
GLA_DK = 256
GLA_DV = 512
GLA_RANK = 16
GLA_GATE_NORM = 16.0
GLA_MAIN = 2 * GLA_HEADS * GLA_DK + 2 * GLA_HEADS * GLA_DV
GLA_R_PAD = 128

CHUNK = 64
LANE = 128
VMEM_LIMIT = 56 * 1024 * 1024

TOKEN_TILE = 512
FFN_TILE = 512
ROW_GROUPS = 2
SCAN_BLOCK = 1024
ATTN_TQ = 512
ATTN_SPLIT = 2
ATTN_TK = 512
LOG2E = math.log2(math.e)
ATTN_Q_SCALE = MLA_QK ** -0.5 * LOG2E


def _cparams(sem):
    return pltpu.CompilerParams(dimension_semantics=sem, vmem_limit_bytes=VMEM_LIMIT)


def _silu(x):
    return x * jax.nn.sigmoid(x)


def _log_sigmoid(z):
    return jnp.minimum(z, 0.0) - jnp.log(1.0 + jnp.exp(-jnp.abs(z)))


def _adaln(x, g, scale, shift):
    ms = jnp.mean(x * x, axis=-1, keepdims=True)
    y = x * lax.rsqrt(ms + EPS) * g
    return y * (1.0 + scale) + shift


def _dot(a, b):
    return jnp.dot(a, b, preferred_element_type=F32)


def _dot_nt(a, b):
    return lax.dot_general(a, b, (((1,), (1,)), ((), ())), preferred_element_type=F32)


def _dot_tn(a, b):
    return lax.dot_general(a, b, (((0,), (0,)), ((), ())), preferred_element_type=F32)


def _ada_kernel(c_ref, w_ref, b_ref, o_ref):
    cond = _silu(c_ref[...]).astype(BF16)
    o_ref[0] = _dot(cond, w_ref[0].astype(BF16)) + b_ref[0]


def _ada_mod(c_all, ada_w, ada_b):
    nb = c_all.shape[0]
    n_out = ada_w.shape[-1]
    tn = 1024
    return pl.pallas_call(
        _ada_kernel,
        grid=(DEPTH, n_out // tn),
        in_specs=[
            pl.BlockSpec((nb, D_MODEL), lambda l, j: (0, 0)),
            pl.BlockSpec((1, D_MODEL, tn), lambda l, j: (l, 0, j)),
            pl.BlockSpec((1, 1, tn), lambda l, j: (l, 0, j)),
        ],
        out_specs=pl.BlockSpec((1, nb, tn), lambda l, j: (l, 0, j)),
        out_shape=jax.ShapeDtypeStruct((DEPTH, nb, n_out), F32),
        compiler_params=_cparams(("arbitrary", "arbitrary")),
        name="ada_mod",
    )(c_all, ada_w, ada_b.reshape(DEPTH, 1, n_out))


class _Mod:
    def __init__(self, table, norm_g, nb_total, boff, tiles_per_seq):
        self.table = table
        self.norm_g = norm_g
        self.nb_total = nb_total
        self.boff = boff
        self.tps = tiles_per_seq

    def spec(self, layer, sub, kind):
        nb, boff, tps = self.nb_total, self.boff, self.tps
        return pl.BlockSpec(
            (1, 1, D_MODEL),
            lambda i, *_: (((layer * nb + boff + i // tps) * 9 + sub * 3 + kind), 0, 0))

    def norm_spec(self, layer, sub):
        return pl.BlockSpec((1, 1, D_MODEL), lambda i, *_: (layer * N_SUB + sub, 0, 0))


def _ffn_kernel(x_ref, sh_ref, sc_ref, gt_ref, ng_ref, w1_ref, w3_ref, w2_ref, o_ref, h_ref, acc_ref, *, nj, ngroups):
    j = pl.program_id(1)
    rg = x_ref.shape[0] // ngroups

    def body(first, last):
        w1, w3, w2 = w1_ref[...], w3_ref[...], w2_ref[...]
        ups = []
        for g in range(ngroups):
            rows = slice(g * rg, (g + 1) * rg)
            if first:
                h = _adaln(x_ref[rows, :], ng_ref[0], sc_ref[0], sh_ref[0]).astype(BF16)
                h_ref[rows, :] = h
            else:
                h = h_ref[rows, :]
            ups.append((_dot(h, w1), _dot(h, w3)))
        for g, (a, u) in enumerate(ups):
            rows = slice(g * rg, (g + 1) * rg)
            down = _dot((_silu(a) * u).astype(BF16), w2)
            acc = down if first else acc_ref[rows, :] + down
            if last:
                o_ref[rows, :] = x_ref[rows, :] + (0.5 * gt_ref[0]) * acc
            else:
                acc_ref[rows, :] = acc

    pl.when(j == 0)(lambda: body(True, False))
    pl.when(jnp.logical_and(j > 0, j < nj - 1))(lambda: body(False, False))
    pl.when(j == nj - 1)(lambda: body(False, True))


def _ffn(x, mod, layer, sub, which, w13, w2):
    n = x.shape[0]
    tm, tf = TOKEN_TILE, FFN_TILE
    nj = FFN_DIM // tf
    return pl.pallas_call(
        functools.partial(_ffn_kernel, nj=nj, ngroups=ROW_GROUPS),
        grid=(n // tm, nj),
        in_specs=[
            pl.BlockSpec((tm, D_MODEL), lambda i, j: (i, 0)),
            mod.spec(layer, sub, 0), mod.spec(layer, sub, 1), mod.spec(layer, sub, 2),
            mod.norm_spec(layer, sub),
            pl.BlockSpec((None, None, D_MODEL, tf), lambda i, j: (layer, which, 0, j)),
            pl.BlockSpec((None, None, D_MODEL, tf), lambda i, j: (layer, which, 0, j + nj)),
            pl.BlockSpec((None, None, tf, D_MODEL), lambda i, j: (layer, which, j, 0)),
        ],
        out_specs=pl.BlockSpec((tm, D_MODEL), lambda i, j: (i, 0)),
        out_shape=jax.ShapeDtypeStruct((n, D_MODEL), F32),
        scratch_shapes=[pltpu.VMEM((tm, D_MODEL), BF16), pltpu.VMEM((tm, D_MODEL), F32)],
        compiler_params=_cparams(("arbitrary", "arbitrary")),
        name="ffn",
    )(x, mod.table, mod.table, mod.table, mod.norm_g, w13, w13, w2)


def _inproj_kernel(x_ref, sh_ref, sc_ref, ng_ref, wm_ref, ws_ref, om_ref, os_ref, h_ref, *, ngroups):
    j = pl.program_id(1)
    rg = x_ref.shape[0] // ngroups

    @pl.when(j == 0)
    def _():
        for g in range(ngroups):
            rows = slice(g * rg, (g + 1) * rg)
            h = _adaln(x_ref[rows, :], ng_ref[0], sc_ref[0], sh_ref[0]).astype(BF16)
            h_ref[rows, :] = h
            om_ref[rows, :] = _dot(h, wm_ref[...])
            os_ref[rows, :] = _dot(h, ws_ref[...])

    @pl.when(j > 0)
    def _():
        om_ref[...] = _dot(h_ref[...], wm_ref[...])


def _inproj(x, mod, layer, w_main, w_small, tn):
    n = x.shape[0]
    tm = TOKEN_TILE
    n_main, n_small = w_main.shape[1], w_small.shape[1]
    return pl.pallas_call(
        functools.partial(_inproj_kernel, ngroups=ROW_GROUPS),
        grid=(n // tm, n_main // tn),
        in_specs=[
            pl.BlockSpec((tm, D_MODEL), lambda i, j: (i, 0)),
            mod.spec(layer, 1, 0), mod.spec(layer, 1, 1),
            mod.norm_spec(layer, 1),
            pl.BlockSpec((D_MODEL, tn), lambda i, j: (0, j)),
            pl.BlockSpec((D_MODEL, n_small), lambda i, j: (0, 0)),
        ],
        out_specs=[
            pl.BlockSpec((tm, tn), lambda i, j: (i, j)),
            pl.BlockSpec((tm, n_small), lambda i, j: (i, 0)),
        ],
        out_shape=[jax.ShapeDtypeStruct((n, n_main), F32), jax.ShapeDtypeStruct((n, n_small), F32)],
        scratch_shapes=[pltpu.VMEM((tm, D_MODEL), BF16)],
        compiler_params=_cparams(("arbitrary", "arbitrary")),
        name="inproj",
    )(x, mod.table, mod.table, mod.norm_g, w_main, w_small)


def _outproj_kernel(x_ref, gt_ref, *refs):
    o_ref = refs[-1]
    npair = (len(refs) - 1) // 2
    y = _dot(refs[0][...], refs[npair][...])
    for p in range(1, npair):
        y = y + _dot(refs[p][...], refs[npair + p][...])
    o_ref[...] = x_ref[...] + gt_ref[0] * y


def _outproj(x, mod, layer, mixes, ws):
    n = x.shape[0]
    tm = TOKEN_TILE
    in_specs = [pl.BlockSpec((tm, D_MODEL), lambda i: (i, 0)), mod.spec(layer, 1, 2)]
    in_specs += [pl.BlockSpec((tm, m.shape[1]), lambda i: (i, 0)) for m in mixes]
    in_specs += [pl.BlockSpec(w.shape, lambda i: (0, 0)) for w in ws]
    return pl.pallas_call(
        _outproj_kernel,
        grid=(n // tm,),
        in_specs=in_specs,
        out_specs=pl.BlockSpec((tm, D_MODEL), lambda i: (i, 0)),
        out_shape=jax.ShapeDtypeStruct((n, D_MODEL), F32),
        compiler_params=_cparams(("arbitrary",)),
        name="outproj",
    )(x, mod.table, *mixes, *ws)


def _scan_consts(c):
    nlev = int(math.log2(c))
    t = np.arange(c)
    row, col = t[:, None], t[None, :]
    blocks = [col <= row]
    masks = [np.eye(c, dtype=bool)]
    for lev in range(nlev):
        s = 1 << lev
        blk = t // s
        odd = (blk % 2) == 1
        bstart = (blk * s)[:, None]
        bend = bstart + s - 1
        as_query = (col >= bstart) & (col <= row)
        as_key = (col > row) & (col <= bend)
        blocks.append(np.where(odd[:, None], as_query, as_key))
        masks.append(odd[:, None] & (~odd[None, :]) & ((row // (2 * s)) == (col // (2 * s))))
    ones = np.ones((8, c), dtype=bool)
    m_f = np.concatenate(blocks + [ones], axis=0)
    m_b = np.concatenate([b[::-1, ::-1] for b in blocks] + [ones], axis=0)
    msum = jnp.asarray(np.stack([np.tile(m_f, (1, 2)), np.tile(m_b, (1, 2))]).astype(np.float32), dtype=BF16)
    mask = jnp.asarray(np.stack([np.stack(masks), np.stack([m[::-1, ::-1] for m in masks])]).astype(np.float32))
    return msum, mask, nlev


def _split2(g):
    hi = g.astype(BF16)
    lo = (g - hi.astype(F32)).astype(BF16)
    return hi, lo


def _scan_block(backward, gate_input, load_chunk, emit, msum_ref, mask_ref, st_ref, *, nch, c, nlev):
    msum = msum_ref[0]
    masks = [mask_ref[0, i] > 0.5 for i in range(nlev + 1)]
    starts = [((nch - 1 - ci) if backward else ci) * c for ci in range(nch)]
    order = [slice(r0, r0 + c) for r0 in starts]

    def stage1(rows, z):
        q, k, v, g = load_chunk(rows, z)
        hi, lo = _split2(g * LOG2E)
        return q, k, v.astype(BF16), _dot(msum, jnp.concatenate([hi, lo], axis=0))

    def stage2(q, k, vb, x):
        e = jnp.exp2(x)
        tot = x[(1 + nlev) * c:(1 + nlev) * c + 1]
        dtot = e[(1 + nlev) * c:(1 + nlev) * c + 1]
        qe = (q * e[0:c]).astype(BF16)
        ke = (k * jnp.exp2(tot - x[0:c])).astype(BF16)
        att = jnp.where(masks[0], _dot_nt(q.astype(BF16), k.astype(BF16)), 0.0)
        for lev in range(nlev):
            es = e[(1 + lev) * c:(2 + lev) * c]
            att = jnp.where(masks[1 + lev], _dot_nt((q * es).astype(BF16), (k * es).astype(BF16)), att)
        return qe, dtot, att.astype(BF16), vb, _dot_tn(vb, ke)

    def stage3(rows, st, qe, dtot, att, vb, inc):
        emit(rows, _dot(att, vb) + _dot_nt(qe, st.astype(BF16)))
        return st * dtot + inc

    zs, s1, s2 = {}, {}, {}
    st = st_ref[...]
    for step in range(nch + 3):
        if step < nch:
            zs[step] = gate_input(order[step])
        if 0 <= step - 1 < nch:
            s1[step - 1] = stage1(order[step - 1], zs.pop(step - 1))
        if 0 <= step - 2 < nch:
            s2[step - 2] = stage2(*s1.pop(step - 2))
        if 0 <= step - 3 < nch:
            st = stage3(order[step - 3], st, *s2.pop(step - 3))
    st_ref[...] = st


def _scan_both_directions(ph, blk, tb, gate_input, load_chunk, ofwd_ref, o_ref, on_ref, gate_ref, **kw):
    c = kw["c"]

    def seq_rows(rows):
        return pl.ds(pl.multiple_of(blk * tb + rows.start, c), c)

    def emit_fwd(rows, o):
        ofwd_ref[seq_rows(rows), :] = o

    def emit_bwd(rows, o):
        o_sum = ofwd_ref[seq_rows(rows), :] + o
        ms = jnp.mean(o_sum * o_sum, axis=-1, keepdims=True)
        o_ref[0, rows, :] = (o_sum * lax.rsqrt(ms + EPS) * on_ref[...] * _silu(gate_ref[0, rows, :])).astype(BF16)

    pl.when(ph == 0)(lambda: _scan_block(False, gate_input, load_chunk, emit_fwd, **kw))
    pl.when(ph == 1)(lambda: _scan_block(True, gate_input, load_chunk, emit_bwd, **kw))


def _hgrn_kernel(q_ref, z_ref, v_ref, gate_ref, lb_ref, on_ref, msum_ref, mask_ref, o_ref, ofwd_ref, st_ref,
                 *, nblk, tb, c, nlev):
    ph = pl.program_id(2)
    jb = pl.program_id(3)
    blk = jnp.where(ph == 0, jb, nblk - 1 - jb)

    @pl.when(jb == 0)
    def _():
        st_ref[...] = jnp.zeros_like(st_ref)

    log_lb = lb_ref[0, 0, 0:1, :]
    log_1mlb = lb_ref[0, 0, 1:2, :]
    one_mlb = lb_ref[0, 0, 2:3, :]

    def gate_input(rows):
        return z_ref[0, rows, :]

    def load_chunk(rows, z):
        q = _silu(q_ref[0, rows, :])
        t = jnp.exp(-jnp.abs(z))
        d = 1.0 + t
        a = log_lb
        b = log_1mlb + (jnp.minimum(z, 0.0) - jnp.log(d))
        g = jnp.maximum(a, b) + jnp.log(1.0 + jnp.exp(-jnp.abs(a - b)))
        k = one_mlb * (jnp.where(z > 0.0, t, 1.0) / d)
        return q, k, v_ref[0, rows, :], g

    _scan_both_directions(ph, blk, tb, gate_input, load_chunk, ofwd_ref, o_ref, on_ref, gate_ref,
                          msum_ref=msum_ref, mask_ref=mask_ref, st_ref=st_ref, nch=tb // c, c=c, nlev=nlev)


def _hgrn_scan(u, lbp, onorm, b, t):
    tb, c = min(SCAN_BLOCK, t), CHUNK
    nblk = t // tb
    msum, mask, nlev = _scan_consts(c)
    nh = HG_HEADS

    def blk_of(ph, jb):
        return jnp.where(ph == 0, jb, nblk - 1 - jb)

    return pl.pallas_call(
        functools.partial(_hgrn_kernel, nblk=nblk, tb=tb, c=c, nlev=nlev),
        grid=(b, nh, 2, nblk),
        in_specs=[
            pl.BlockSpec((1, tb, HG_D), lambda bi, h, ph, jb: (bi, blk_of(ph, jb), h)),
            pl.BlockSpec((1, tb, HG_D), lambda bi, h, ph, jb: (bi, blk_of(ph, jb), nh + nh * ph + h)),
            pl.BlockSpec((1, tb, HG_D), lambda bi, h, ph, jb: (bi, blk_of(ph, jb), 3 * nh + h)),
            pl.BlockSpec((1, tb, HG_D),
                         lambda bi, h, ph, jb: (bi, jnp.where(ph == 0, nblk - 1, nblk - 1 - jb), 4 * nh + h)),
            pl.BlockSpec((1, 1, 8, HG_D), lambda bi, h, ph, jb: (ph, h, 0, 0)),
            pl.BlockSpec((1, HG_D), lambda bi, h, ph, jb: (0, 0)),
            pl.BlockSpec((1,) + msum.shape[1:], lambda bi, h, ph, jb: (ph, 0, 0)),
            pl.BlockSpec((1,) + mask.shape[1:], lambda bi, h, ph, jb: (ph, 0, 0, 0)),
        ],
        out_specs=pl.BlockSpec(
            (1, tb, HG_D), lambda bi, h, ph, jb: (bi, jnp.where(ph == 0, nblk - 1, nblk - 1 - jb), h)),
        out_shape=jax.ShapeDtypeStruct((b, t, HG_WIDTH), BF16),
        scratch_shapes=[pltpu.VMEM((t, HG_D), F32), pltpu.VMEM((HG_D, HG_D), F32)],
        compiler_params=_cparams(("arbitrary",) * 4),
        name="hgrn_scan",
    )(u, u, u, u, lbp, onorm, msum, mask)


def _gla_kernel(q_ref, k_ref, v_ref, gate_ref, r_ref, w2_ref, gb_ref, on_ref, msum_ref, mask_ref, o_ref,
                ofwd_ref, st_ref, *, nblk, tb, c, nlev):
    ph = pl.program_id(2)
    jb = pl.program_id(3)
    blk = jnp.where(ph == 0, jb, nblk - 1 - jb)

    @pl.when(jb == 0)
    def _():
        st_ref[...] = jnp.zeros_like(st_ref)

    w2 = w2_ref[0, 0]
    gb = gb_ref[0, 0]

    def gate_input(rows):
        return _dot(r_ref[0, rows, :].astype(BF16), w2) + gb

    def load_chunk(rows, z):
        q = q_ref[0, rows, :] * (GLA_DK ** -0.5)
        g = _log_sigmoid(z) * (1.0 / GLA_GATE_NORM)
        return q, k_ref[0, rows, :], v_ref[0, rows, :], g

    _scan_both_directions(ph, blk, tb, gate_input, load_chunk, ofwd_ref, o_ref, on_ref, gate_ref,
                          msum_ref=msum_ref, mask_ref=mask_ref, st_ref=st_ref, nch=tb // c, c=c, nlev=nlev)


def _gla_scan(u, r, w2p, gbias, onorm, b, t):
    tb, c = min(SCAN_BLOCK, t), CHUNK
    nblk = t // tb
    msum, mask, nlev = _scan_consts(c)
    nh = GLA_HEADS
    vblk0 = 2 * nh * GLA_DK // GLA_DV

    def blk_of(ph, jb):
        return jnp.where(ph == 0, jb, nblk - 1 - jb)

    return pl.pallas_call(
        functools.partial(_gla_kernel, nblk=nblk, tb=tb, c=c, nlev=nlev),
        grid=(b, nh, 2, nblk),
        in_specs=[
            pl.BlockSpec((1, tb, GLA_DK), lambda bi, h, ph, jb: (bi, blk_of(ph, jb), h)),
            pl.BlockSpec((1, tb, GLA_DK), lambda bi, h, ph, jb: (bi, blk_of(ph, jb), nh + h)),
            pl.BlockSpec((1, tb, GLA_DV), lambda bi, h, ph, jb: (bi, blk_of(ph, jb), vblk0 + h)),
            pl.BlockSpec((1, tb, GLA_DV),
                         lambda bi, h, ph, jb: (bi, jnp.where(ph == 0, nblk - 1, nblk - 1 - jb), vblk0 + nh + h)),
            pl.BlockSpec((1, tb, GLA_R_PAD), lambda bi, h, ph, jb: (bi, blk_of(ph, jb), 0)),
            pl.BlockSpec((1, 1, GLA_R_PAD, GLA_DK), lambda bi, h, ph, jb: (ph, h, 0, 0)),
            pl.BlockSpec((1, 1, 1, GLA_DK), lambda bi, h, ph, jb: (ph, h, 0, 0)),
            pl.BlockSpec((1, GLA_DV), lambda bi, h, ph, jb: (0, 0)),
            pl.BlockSpec((1,) + msum.shape[1:], lambda bi, h, ph, jb: (ph, 0, 0)),
            pl.BlockSpec((1,) + mask.shape[1:], lambda bi, h, ph, jb: (ph, 0, 0, 0)),
        ],
        out_specs=pl.BlockSpec(
            (1, tb, GLA_DV), lambda bi, h, ph, jb: (bi, jnp.where(ph == 0, nblk - 1, nblk - 1 - jb), h)),
        out_shape=jax.ShapeDtypeStruct((b, t, nh * GLA_DV), BF16),
        scratch_shapes=[pltpu.VMEM((t, GLA_DV), F32), pltpu.VMEM((GLA_DV, GLA_DK), F32)],
        compiler_params=_cparams(("arbitrary",) * 4),
        name="gla_scan",
    )(u, u, u, u, r, w2p, gbias, onorm, msum, mask)


def _rope128(x, cos, sin):
    lane = lax.broadcasted_iota(jnp.int32, x.shape, 1)
    half = MLA_ROPE // 2
    swapped = jnp.where(lane < half, pltpu.roll(x, LANE - half, 1), pltpu.roll(x, half, 1))
    return x * cos + swapped * sin


def _mla_prep_kernel(c_ref, cos_ref, sin_ref, qag_ref, kvag_ref, wq_ref, wk_ref, wv_ref, gq_ref, gk_ref,
                     q_ref, k_ref, v_ref):
    c = c_ref[...]
    cq = c[:, :Q_LORA]
    ckv = c[:, Q_LORA:Q_LORA + KV_LORA]
    kpe = c[:, Q_LORA + KV_LORA:]
    cqn = (cq * lax.rsqrt(jnp.mean(cq * cq, axis=-1, keepdims=True) + EPS) * qag_ref[...]).astype(BF16)
    ckvn = (ckv * lax.rsqrt(jnp.mean(ckv * ckv, axis=-1, keepdims=True) + EPS) * kvag_ref[...]).astype(BF16)
    cos = cos_ref[...]
    sin = sin_ref[...]
    gq = gq_ref[...]
    gk = gk_ref[...]
    q_raw = _dot(cqn, wq_ref[...])
    kn_raw = _dot(ckvn, wk_ref[...])
    v_ref[...] = _dot(ckvn, wv_ref[...]).astype(BF16)
    kpe_ss = jnp.sum(kpe * kpe, axis=-1, keepdims=True)
    kpe_rot = _rope128(kpe * gk[:, LANE:], cos, sin)
    for h in range(MLA_HEADS):
        lo = h * MLA_QK_PAD
        qn = q_raw[:, lo:lo + LANE]
        qr = q_raw[:, lo + LANE:lo + 2 * LANE]
        ss = jnp.sum(qn * qn, axis=-1, keepdims=True) + jnp.sum(qr * qr, axis=-1, keepdims=True)
        rinv = lax.rsqrt(ss * (1.0 / MLA_QK) + EPS) * ATTN_Q_SCALE
        q_ref[:, lo:lo + LANE] = (qn * rinv * gq[:, :LANE]).astype(BF16)
        q_ref[:, lo + LANE:lo + 2 * LANE] = _rope128(qr * rinv * gq[:, LANE:], cos, sin).astype(BF16)
        kn = kn_raw[:, h * LANE:(h + 1) * LANE]
        ssk = jnp.sum(kn * kn, axis=-1, keepdims=True) + kpe_ss
        rinvk = lax.rsqrt(ssk * (1.0 / MLA_QK) + EPS)
        k_ref[:, lo:lo + LANE] = (kn * rinvk * gk[:, :LANE]).astype(BF16)
        k_ref[:, lo + LANE:lo + 2 * LANE] = (kpe_rot * rinvk).astype(BF16)


def _mla_prep(c, cos, sin, qag, kvag, wq, wk, wv, gq, gk, t):
    n = c.shape[0]
    tm = min(256, t)
    tps = t // tm
    full = lambda a: pl.BlockSpec(a.shape, lambda i: (0,) * a.ndim)
    return pl.pallas_call(
        _mla_prep_kernel,
        grid=(n // tm,),
        in_specs=[
            pl.BlockSpec((tm, MLA_C_PAD), lambda i: (i, 0)),
            pl.BlockSpec((tm, LANE), lambda i: (i % tps, 0)),
            pl.BlockSpec((tm, LANE), lambda i: (i % tps, 0)),
            full(qag), full(kvag), full(wq), full(wk), full(wv), full(gq), full(gk),
        ],
        out_specs=[
            pl.BlockSpec((tm, MLA_HEADS * MLA_QK_PAD), lambda i: (i, 0)),
            pl.BlockSpec((tm, MLA_HEADS * MLA_QK_PAD), lambda i: (i, 0)),
            pl.BlockSpec((tm, MLA_HEADS * MLA_V), lambda i: (i, 0)),
        ],
        out_shape=[
            jax.ShapeDtypeStruct((n, MLA_HEADS * MLA_QK_PAD), BF16),
            jax.ShapeDtypeStruct((n, MLA_HEADS * MLA_QK_PAD), BF16),
            jax.ShapeDtypeStruct((n, MLA_HEADS * MLA_V), BF16),
        ],
        compiler_params=_cparams(("arbitrary",)),
        name="mla_prep",
    )(c, cos, sin, qag, kvag, wq, wk, wv, gq, gk)


def _attn_kernel(q_ref, k_ref, v_ref, o_ref, *, tk, nk, nsplit):
    tq = q_ref.shape[1]
    th = tq // nsplit
    qs = [q_ref[0, i * th:(i + 1) * th, :] for i in range(nsplit)]

    def scores(ci):
        kc = k_ref[0, ci * tk:(ci + 1) * tk, :]
        return [_dot_nt(q, kc) for q in qs]

    def update(ci, s_list, state):
        vc = v_ref[0, ci * tk:(ci + 1) * tk, :]
        out = []
        for s, (m, l, acc) in zip(s_list, state):
            m_new = jnp.maximum(m, jnp.max(s, axis=-1, keepdims=True))
            alpha = jnp.exp2(m - m_new)
            p = jnp.exp2(s - m_new)
            l = alpha * l + jnp.sum(p, axis=-1, keepdims=True)
            acc = alpha * acc + _dot(p.astype(BF16), vc)
            out.append((m_new, l, acc))
        return out

    state = [(jnp.full((th, 1), -jnp.inf, F32), jnp.zeros((th, 1), F32), jnp.zeros((th, MLA_V), F32))
             for _ in range(nsplit)]
    s_cur = scores(0)
    for ci in range(nk):
        s_next = scores(ci + 1) if ci + 1 < nk else None
        state = update(ci, s_cur, state)
        s_cur = s_next
    for i, (_, l, acc) in enumerate(state):
        o_ref[0, i * th:(i + 1) * th, :] = (acc / l).astype(BF16)


def _attention(q, k, v, b, t):
    tq, tk = min(ATTN_TQ, t), min(ATTN_TK, t)
    return pl.pallas_call(
        functools.partial(_attn_kernel, tk=tk, nk=t // tk, nsplit=ATTN_SPLIT),
        grid=(b, MLA_HEADS, t // tq),
        in_specs=[
            pl.BlockSpec((1, tq, MLA_QK_PAD), lambda bi, h, i: (bi, i, h)),
            pl.BlockSpec((1, t, MLA_QK_PAD), lambda bi, h, i: (bi, 0, h)),
            pl.BlockSpec((1, t, MLA_V), lambda bi, h, i: (bi, 0, h)),
        ],
        out_specs=pl.BlockSpec((1, tq, MLA_V), lambda bi, h, i: (bi, i, h)),
        out_shape=jax.ShapeDtypeStruct((b, t, MLA_HEADS * MLA_V), BF16),
        compiler_params=_cparams(("arbitrary",) * 3),
        name="mla_attention",
    )(q, k, v)


def _rope_tables(t):
    half = MLA_ROPE // 2
    inv_freq = ROPE_THETA ** (-jnp.arange(half, dtype=F32) / half)
    ang = jnp.arange(t, dtype=jnp.int32).astype(F32)[:, None] * inv_freq[None, :]
    cos, sin = jnp.cos(ang), jnp.sin(ang)
    zeros = jnp.zeros((t, LANE - MLA_ROPE), F32)
    return (jnp.concatenate([cos, cos, zeros], axis=-1), jnp.concatenate([-sin, sin, zeros], axis=-1))


def _pad_cols(a, width):
    return jnp.pad(a, [(0, 0)] * (a.ndim - 1) + [(0, width - a.shape[-1])])


def _prep_even(e, w):
    ev_in = w['ev_w_in'][e]
    w_hg = ev_in[:, :5 * HG_WIDTH].astype(BF16)
    w_mla = _pad_cols(ev_in[:, 5 * HG_WIDTH:], MLA_C_PAD).astype(BF16)
    uq = w['mla_w_uq'][e].reshape(Q_LORA, MLA_HEADS, MLA_QK)
    wq = _pad_cols(uq, MLA_QK_PAD).reshape(Q_LORA, MLA_HEADS * MLA_QK_PAD).astype(BF16)
    ukv = w['mla_w_ukv'][e].reshape(KV_LORA, MLA_HEADS, MLA_NOPE + MLA_V)
    wk = ukv[:, :, :MLA_NOPE].reshape(KV_LORA, MLA_HEADS * MLA_NOPE).astype(BF16)
    wv = ukv[:, :, MLA_NOPE:].reshape(KV_LORA, MLA_HEADS * MLA_V).astype(BF16)
    p = jax.nn.softmax(w['hgrn_lb'].astype(F32), axis=1)
    lb = jnp.cumsum(p, axis=1)
    lb = (lb - lb[:, :1])[:, e].reshape(2, HG_HEADS, 1, HG_D)
    lbp = jnp.concatenate([jnp.log(lb), jnp.log1p(-lb), 1.0 - lb, jnp.zeros((2, HG_HEADS, 5, HG_D), F32)], axis=2)
    w_out = w['ev_w_out'][e].astype(BF16)
    return dict(
        w_hg=w_hg, w_mla=w_mla, wq=wq, wk=wk, wv=wv, lbp=lbp,
        hg_onorm=w['hgrn_onorm_g'][e].reshape(1, HG_D),
        qag=w['mla_qa_norm_g'][e].reshape(1, Q_LORA), kvag=w['mla_kva_norm_g'][e].reshape(1, KV_LORA),
        gq=_pad_cols(w['mla_qn_g'][e].reshape(1, MLA_QK), MLA_QK_PAD),
        gk=_pad_cols(w['mla_kn_g'][e].reshape(1, MLA_QK), MLA_QK_PAD),
        w_out_hg=w_out[:HG_WIDTH], w_out_mla=w_out[HG_WIDTH:],
    )


def _prep_odd(e, w):
    od_in = w['od_w_in'][e]
    w_main = od_in[:, :GLA_MAIN].astype(BF16)
    w_r = _pad_cols(od_in[:, GLA_MAIN:], GLA_R_PAD).astype(BF16)
    w2 = w['gla_gk_w2'][e].reshape(2, GLA_RANK, GLA_HEADS, GLA_DK).transpose(0, 2, 1, 3)
    w2p = jnp.zeros((2, GLA_HEADS, GLA_R_PAD, GLA_DK), F32)
    w2p = w2p.at[0, :, :GLA_RANK].set(w2[0]).at[1, :, GLA_RANK:2 * GLA_RANK].set(w2[1]).astype(BF16)
    gbias = w['gla_gk_b'][e].reshape(2, GLA_HEADS, 1, GLA_DK)
    return dict(w_main=w_main, w_r=w_r, w2p=w2p, gbias=gbias,
                onorm=w['gla_onorm_g'][e].reshape(1, GLA_DV), w_out=w['od_w_out'][e].astype(BF16))


def _trunk(x3, boff, nb_total, table, norm_g, w13, w2, evens, odds):
    b, t, _ = x3.shape
    n = b * t
    x = x3.reshape(n, D_MODEL)
    mod = _Mod(table, norm_g, nb_total, boff, t // TOKEN_TILE)
    cos, sin = _rope_tables(t)
    for layer in range(DEPTH):
        x = _ffn(x, mod, layer, 0, 0, w13, w2)
        e = layer // 2
        if layer % 2 == 0:
            p = evens[e]
            u_hg, c_mla = _inproj(x, mod, layer, p['w_hg'], p['w_mla'], 1024)
            o_hg = _hgrn_scan(u_hg.reshape(b, t, 5 * HG_WIDTH), p['lbp'], p['hg_onorm'], b, t)
            q, k, v = _mla_prep(c_mla, cos, sin, p['qag'], p['kvag'], p['wq'], p['wk'], p['wv'], p['gq'], p['gk'], t)
            o_mla = _attention(q.reshape(b, t, -1), k.reshape(b, t, -1), v.reshape(b, t, -1), b, t)
            x = _outproj(x, mod, layer, [o_hg.reshape(n, HG_WIDTH), o_mla.reshape(n, MLA_HEADS * MLA_V)],
                         [p['w_out_hg'], p['w_out_mla']])
        else:
            p = odds[e]
            u, r = _inproj(x, mod, layer, p['w_main'], p['w_r'], 1024)
            o = _gla_scan(u.reshape(b, t, GLA_MAIN), r.reshape(b, t, GLA_R_PAD), p['w2p'], p['gbias'], p['onorm'], b, t)
            x = _outproj(x, mod, layer, [o.reshape(n, GLA_HEADS * GLA_DV)], [p['w_out']])
        x = _ffn(x, mod, layer, 2, 1, w13, w2)
    return x.reshape(b, t, D_MODEL)


def kernel(x_prompt, x_sample, c_prompt, c_sample, ada_w, ada_b, norm_g, ffn_w13, ffn_w2, ev_w_in, ev_w_out, hgrn_lb, hgrn_onorm_g, mla_qa_norm_g, mla_w_uq, mla_kva_norm_g, mla_w_ukv, mla_qn_g, mla_kn_g, od_w_in, od_w_out, gla_gk_w2, gla_gk_b, gla_onorm_g):
    w = dict(ev_w_in=ev_w_in, ev_w_out=ev_w_out, hgrn_lb=hgrn_lb, hgrn_onorm_g=hgrn_onorm_g,
             mla_qa_norm_g=mla_qa_norm_g, mla_w_uq=mla_w_uq, mla_kva_norm_g=mla_kva_norm_g,
             mla_w_ukv=mla_w_ukv, mla_qn_g=mla_qn_g, mla_kn_g=mla_kn_g, od_w_in=od_w_in,
             od_w_out=od_w_out, gla_gk_w2=gla_gk_w2, gla_gk_b=gla_gk_b, gla_onorm_g=gla_onorm_g)
    bp, bs = x_prompt.shape[0], x_sample.shape[0]
    nb_total = bp + bs
    c_all = jnp.concatenate([c_prompt, c_sample], axis=0)
    mod = _ada_mod(c_all, ada_w, ada_b)
    table = mod.reshape(DEPTH * nb_total * 3 * N_SUB, 1, D_MODEL)
    ng = norm_g.reshape(DEPTH * N_SUB, 1, D_MODEL)
    w13 = ffn_w13.astype(BF16)
    w2 = ffn_w2.astype(BF16)
    evens = [_prep_even(e, w) for e in range((DEPTH + 1) // 2)]
    odds = [_prep_odd(e, w) for e in range(DEPTH // 2)]
    y_prompt = _trunk(x_prompt, 0, nb_total, table, ng, w13, w2, evens, odds)
    y_sample = _trunk(x_sample, bp, nb_total, table, ng, w13, w2, evens, odds)
    return (y_prompt, y_sample)
```

```python
import functools
import math

import numpy as np
import jax
import jax.numpy as jnp
from jax import lax
from jax.experimental import pallas as pl
from jax.experimental.pallas import tpu as pltpu

F32 = jnp.float32
BF16 = jnp.bfloat16

D_MODEL = 2048
DEPTH = 4
N_SUB = 3
EPS = 1e-6
FFN_DIM = 5632

HG_HEADS = 8
HG_D = 128
HG_WIDTH = HG_HEADS * HG_D

MLA_HEADS = 8
MLA_NOPE = 128
MLA_ROPE = 64
MLA_QK = MLA_NOPE + MLA_ROPE
MLA_V = 128
MLA_QK_PAD = 256
Q_LORA = 512
KV_LORA = 512
MLA_C_PAD = 1152
ROPE_THETA = 10000.0

GLA_HEADS = 4
GLA_DK = 256
GLA_DV = 512
GLA_RANK = 16
GLA_GATE_NORM = 16.0
GLA_MAIN = 2 * GLA_HEADS * GLA_DK + 2 * GLA_HEADS * GLA_DV
GLA_R_PAD = 128

CHUNK = 64
LANE = 128
VMEM_LIMIT = 56 * 1024 * 1024

TOKEN_TILE = 512
FFN_TILE = 512
INPROJ_TILE = 1024
ROW_GROUPS = 2
SCAN_BLOCK = 1024
ATTN_TQ = 512
ATTN_SPLIT = 2
ATTN_TK = 512
LOG2E = math.log2(math.e)
ATTN_Q_SCALE = MLA_QK ** -0.5 * LOG2E


def _cparams(sem):
    return pltpu.CompilerParams(dimension_semantics=sem, vmem_limit_bytes=VMEM_LIMIT)


def _silu(x):
    return x * jax.nn.sigmoid(x)


def _log_sigmoid(z):
    return jnp.minimum(z, 0.0) - jnp.log(1.0 + jnp.exp(-jnp.abs(z)))


def _adaln(x, g, scale, shift):
    ms = jnp.mean(x * x, axis=-1, keepdims=True)
    y = x * lax.rsqrt(ms + EPS) * g
    return y * (1.0 + scale) + shift


def _dot(a, b):
    return jnp.dot(a, b, preferred_element_type=F32)


def _dot_nt(a, b):
    return lax.dot_general(a, b, (((1,), (1,)), ((), ())), preferred_element_type=F32)


def _dot_tn(a, b):
    return lax.dot_general(a, b, (((0,), (0,)), ((), ())), preferred_element_type=F32)


def _ada_kernel(c_ref, w_ref, b_ref, o_ref):
    cond = _silu(c_ref[...]).astype(BF16)
    o_ref[0] = _dot(cond, w_ref[0].astype(BF16)) + b_ref[0]


def _ada_mod(c_all, ada_w, ada_b):
    nb = c_all.shape[0]
    n_out = ada_w.shape[-1]
    tn = 1024
    return pl.pallas_call(
        _ada_kernel,
        grid=(DEPTH, n_out // tn),
        in_specs=[
            pl.BlockSpec((nb, D_MODEL), lambda l, j: (0, 0)),
            pl.BlockSpec((1, D_MODEL, tn), lambda l, j: (l, 0, j)),
            pl.BlockSpec((1, 1, tn), lambda l, j: (l, 0, j)),
        ],
        out_specs=pl.BlockSpec((1, nb, tn), lambda l, j: (l, 0, j)),
        out_shape=jax.ShapeDtypeStruct((DEPTH, nb, n_out), F32),
        compiler_params=_cparams(("arbitrary", "arbitrary")),
        name="ada_mod",
    )(c_all, ada_w, ada_b.reshape(DEPTH, 1, n_out))


class _Mod:
    def __init__(self, table, norm_g, nb_total, boff, tiles_per_seq):
        self.table = table
        self.norm_g = norm_g
        self.nb_total = nb_total
        self.boff = boff
        self.tps = tiles_per_seq

    def spec(self, layer, sub, kind):
        nb, boff, tps = self.nb_total, self.boff, self.tps
        return pl.BlockSpec(
            (1, 1, D_MODEL),
            lambda i, *_: (((layer * nb + boff + i // tps) * 9 + sub * 3 + kind), 0, 0))

    def norm_spec(self, layer, sub):
        return pl.BlockSpec((1, 1, D_MODEL), lambda i, *_: (layer * N_SUB + sub, 0, 0))


def _ffn_kernel(x_ref, sh_ref, sc_ref, gt_ref, ng_ref, w1_ref, w3_ref, w2_ref, o_ref, h_ref, acc_ref, *, nj, ngroups):
    j = pl.program_id(1)
    rg = x_ref.shape[0] // ngroups

    def body(first, last):
        w1, w3, w2 = w1_ref[...], w3_ref[...], w2_ref[...]
        ups = []
        for g in range(ngroups):
            rows = slice(g * rg, (g + 1) * rg)
            if first:
                h = _adaln(x_ref[rows, :], ng_ref[0], sc_ref[0], sh_ref[0]).astype(BF16)
                h_ref[rows, :] = h
            else:
                h = h_ref[rows, :]
            ups.append((_dot(h, w1), _dot(h, w3)))
        for g, (a, u) in enumerate(ups):
            rows = slice(g * rg, (g + 1) * rg)
            down = _dot((_silu(a) * u).astype(BF16), w2)
            acc = down if first else acc_ref[rows, :] + down
            if last:
                o_ref[rows, :] = x_ref[rows, :] + (0.5 * gt_ref[0]) * acc
            else:
                acc_ref[rows, :] = acc

    pl.when(j == 0)(lambda: body(True, False))
    pl.when(jnp.logical_and(j > 0, j < nj - 1))(lambda: body(False, False))
    pl.when(j == nj - 1)(lambda: body(False, True))


def _ffn(x, mod, layer, sub, which, w13, w2):
    n = x.shape[0]
    tm, tf = TOKEN_TILE, FFN_TILE
    nj = FFN_DIM // tf
    return pl.pallas_call(
        functools.partial(_ffn_kernel, nj=nj, ngroups=ROW_GROUPS),
        grid=(n // tm, nj),
        in_specs=[
            pl.BlockSpec((tm, D_MODEL), lambda i, j: (i, 0)),
            mod.spec(layer, sub, 0), mod.spec(layer, sub, 1), mod.spec(layer, sub, 2),
            mod.norm_spec(layer, sub),
            pl.BlockSpec((None, None, None, D_MODEL, tf), lambda i, j: (layer, which, j, 0, 0)),
            pl.BlockSpec((None, None, None, D_MODEL, tf), lambda i, j: (layer, which, j + nj, 0, 0)),
            pl.BlockSpec((None, None, tf, D_MODEL), lambda i, j: (layer, which, j, 0)),
        ],
        out_specs=pl.BlockSpec((tm, D_MODEL), lambda i, j: (i, 0)),
        out_shape=jax.ShapeDtypeStruct((n, D_MODEL), F32),
        scratch_shapes=[pltpu.VMEM((tm, D_MODEL), BF16), pltpu.VMEM((tm, D_MODEL), F32)],
        compiler_params=_cparams(("arbitrary", "arbitrary")),
        name="ffn",
    )(x, mod.table, mod.table, mod.table, mod.norm_g, w13, w13, w2)


def _inproj_kernel(x_ref, sh_ref, sc_ref, ng_ref, wm_ref, ws_ref, om_ref, os_ref, h_ref, *, ngroups):
    j = pl.program_id(1)
    rg = x_ref.shape[0] // ngroups

    @pl.when(j == 0)
    def _():
        for g in range(ngroups):
            rows = slice(g * rg, (g + 1) * rg)
            h = _adaln(x_ref[rows, :], ng_ref[0], sc_ref[0], sh_ref[0]).astype(BF16)
            h_ref[rows, :] = h
            om_ref[rows, :] = _dot(h, wm_ref[...])
            os_ref[rows, :] = _dot(h, ws_ref[...])

    @pl.when(j > 0)
    def _():
        om_ref[...] = _dot(h_ref[...], wm_ref[...])


def _inproj(x, mod, layer, w_main, w_small):
    n = x.shape[0]
    tm = TOKEN_TILE
    ntiles, _, tn = w_main.shape
    n_main, n_small = ntiles * tn, w_small.shape[1]
    return pl.pallas_call(
        functools.partial(_inproj_kernel, ngroups=ROW_GROUPS),
        grid=(n // tm, ntiles),
        in_specs=[
            pl.BlockSpec((tm, D_MODEL), lambda i, j: (i, 0)),
            mod.spec(layer, 1, 0), mod.spec(layer, 1, 1),
            mod.norm_spec(layer, 1),
            pl.BlockSpec((None, D_MODEL, tn), lambda i, j: (j, 0, 0)),
            pl.BlockSpec((D_MODEL, n_small), lambda i, j: (0, 0)),
        ],
        out_specs=[
            pl.BlockSpec((tm, tn), lambda i, j: (i, j)),
            pl.BlockSpec((tm, n_small), lambda i, j: (i, 0)),
        ],
        out_shape=[jax.ShapeDtypeStruct((n, n_main), F32), jax.ShapeDtypeStruct((n, n_small), F32)],
        scratch_shapes=[pltpu.VMEM((tm, D_MODEL), BF16)],
        compiler_params=_cparams(("arbitrary", "arbitrary")),
        name="inproj",
    )(x, mod.table, mod.table, mod.norm_g, w_main, w_small)


def _outproj_kernel(x_ref, gt_ref, *refs):
    o_ref = refs[-1]
    npair = (len(refs) - 1) // 2
    y = _dot(refs[0][...], refs[npair][...])
    for p in range(1, npair):
        y = y + _dot(refs[p][...], refs[npair + p][...])
    o_ref[...] = x_ref[...] + gt_ref[0] * y


def _outproj(x, mod, layer, mixes, ws):
    n = x.shape[0]
    tm = TOKEN_TILE
    in_specs = [pl.BlockSpec((tm, D_MODEL), lambda i: (i, 0)), mod.spec(layer, 1, 2)]
    in_specs += [pl.BlockSpec((tm, m.shape[1]), lambda i: (i, 0)) for m in mixes]
    in_specs += [pl.BlockSpec(w.shape, lambda i: (0, 0)) for w in ws]
    return pl.pallas_call(
        _outproj_kernel,
        grid=(n // tm,),
        in_specs=in_specs,
        out_specs=pl.BlockSpec((tm, D_MODEL), lambda i: (i, 0)),
        out_shape=jax.ShapeDtypeStruct((n, D_MODEL), F32),
        compiler_params=_cparams(("arbitrary",)),
        name="outproj",
    )(x, mod.table, *mixes, *ws)


def _scan_consts(c):
    nlev = int(math.log2(c))
    t = np.arange(c)
    row, col = t[:, None], t[None, :]
    blocks = [col <= row]
    masks = [np.eye(c, dtype=bool)]
    for lev in range(nlev):
        s = 1 << lev
        blk = t // s
        odd = (blk % 2) == 1
        bstart = (blk * s)[:, None]
        bend = bstart + s - 1
        as_query = (col >= bstart) & (col <= row)
        as_key = (col > row) & (col <= bend)
        blocks.append(np.where(odd[:, None], as_query, as_key))
        masks.append(odd[:, None] & (~odd[None, :]) & ((row // (2 * s)) == (col // (2 * s))))
    ones = np.ones((8, c), dtype=bool)
    m_f = np.concatenate(blocks + [ones], axis=0)
    m_b = np.concatenate([b[::-1, ::-1] for b in blocks] + [ones], axis=0)
    msum = jnp.asarray(np.stack([np.tile(m_f, (1, 2)), np.tile(m_b, (1, 2))]).astype(np.float32), dtype=BF16)
    mask = jnp.asarray(np.stack([np.stack(masks), np.stack([m[::-1, ::-1] for m in masks])]).astype(np.float32))
    return msum, mask, nlev


def _split2(g):
    hi = g.astype(BF16)
    lo = (g - hi.astype(F32)).astype(BF16)
    return hi, lo


def _scan_block(backward, gate_input, load_chunk, emit, msum_ref, mask_ref, st_ref, *, nch, c, nlev):
    msum = msum_ref[0]
    masks = [mask_ref[0, i] > 0.5 for i in range(nlev + 1)]
    starts = [((nch - 1 - ci) if backward else ci) * c for ci in range(nch)]
    order = [slice(r0, r0 + c) for r0 in starts]

    def stage1(rows, z):
        q, k, v, g = load_chunk(rows, z)
        hi, lo = _split2(g * LOG2E)
        return q, k, v.astype(BF16), _dot(msum, jnp.concatenate([hi, lo], axis=0))

    def stage2(q, k, vb, x):
        e = jnp.exp2(x)
        tot = x[(1 + nlev) * c:(1 + nlev) * c + 1]
        dtot = e[(1 + nlev) * c:(1 + nlev) * c + 1]
        qe = (q * e[0:c]).astype(BF16)
        ke = (k * jnp.exp2(tot - x[0:c])).astype(BF16)
        att = jnp.where(masks[0], _dot_nt(q.astype(BF16), k.astype(BF16)), 0.0)
        for lev in range(nlev):
            es = e[(1 + lev) * c:(2 + lev) * c]
            att = jnp.where(masks[1 + lev], _dot_nt((q * es).astype(BF16), (k * es).astype(BF16)), att)
        return qe, dtot, att.astype(BF16), vb, _dot_tn(vb, ke)

    def stage3(rows, st, qe, dtot, att, vb, inc):
        emit(rows, _dot(att, vb) + _dot_nt(qe, st.astype(BF16)))
        return st * dtot + inc

    zs, s1, s2 = {}, {}, {}
    st = st_ref[...]
    for step in range(nch + 3):
        if step < nch:
            zs[step] = gate_input(order[step])
        if 0 <= step - 1 < nch:
            s1[step - 1] = stage1(order[step - 1], zs.pop(step - 1))
        if 0 <= step - 2 < nch:
            s2[step - 2] = stage2(*s1.pop(step - 2))
        if 0 <= step - 3 < nch:
            st = stage3(order[step - 3], st, *s2.pop(step - 3))
    st_ref[...] = st


def _scan_both_directions(ph, blk, tb, gate_input, load_chunk, ofwd_ref, o_ref, on_ref, gate_ref, **kw):
    c = kw["c"]

    def seq_rows(rows):
        return pl.ds(pl.multiple_of(blk * tb + rows.start, c), c)

    def emit_fwd(rows, o):
        ofwd_ref[seq_rows(rows), :] = o

    def emit_bwd(rows, o):
        o_sum = ofwd_ref[seq_rows(rows), :] + o
        ms = jnp.mean(o_sum * o_sum, axis=-1, keepdims=True)
        o_ref[0, rows, :] = (o_sum * lax.rsqrt(ms + EPS) * on_ref[...] * _silu(gate_ref[0, rows, :])).astype(BF16)

    pl.when(ph == 0)(lambda: _scan_block(False, gate_input, load_chunk, emit_fwd, **kw))
    pl.when(ph == 1)(lambda: _scan_block(True, gate_input, load_chunk, emit_bwd, **kw))


def _hgrn_kernel(q_ref, z_ref, v_ref, gate_ref, lb_ref, on_ref, msum_ref, mask_ref, o_ref, ofwd_ref, st_ref,
                 *, nblk, tb, c, nlev):
    ph = pl.program_id(2)
    jb = pl.program_id(3)
    blk = jnp.where(ph == 0, jb, nblk - 1 - jb)

    @pl.when(jb == 0)
    def _():
        st_ref[...] = jnp.zeros_like(st_ref)

    log_lb = lb_ref[0, 0, 0:1, :]
    log_1mlb = lb_ref[0, 0, 1:2, :]
    one_mlb = lb_ref[0, 0, 2:3, :]

    def gate_input(rows):
        return z_ref[0, rows, :]

    def load_chunk(rows, z):
        q = _silu(q_ref[0, rows, :])
        t = jnp.exp(-jnp.abs(z))
        d = 1.0 + t
        a = log_lb
        b = log_1mlb + (jnp.minimum(z, 0.0) - jnp.log(d))
        g = jnp.maximum(a, b) + jnp.log(1.0 + jnp.exp(-jnp.abs(a - b)))
        k = one_mlb * (jnp.where(z > 0.0, t, 1.0) / d)
        return q, k, v_ref[0, rows, :], g

    _scan_both_directions(ph, blk, tb, gate_input, load_chunk, ofwd_ref, o_ref, on_ref, gate_ref,
                          msum_ref=msum_ref, mask_ref=mask_ref, st_ref=st_ref, nch=tb // c, c=c, nlev=nlev)


def _hgrn_scan(u, lbp, onorm, b, t):
    tb, c = min(SCAN_BLOCK, t), CHUNK
    nblk = t // tb
    msum, mask, nlev = _scan_consts(c)
    nh = HG_HEADS

    def blk_of(ph, jb):
        return jnp.where(ph == 0, jb, nblk - 1 - jb)

    return pl.pallas_call(
        functools.partial(_hgrn_kernel, nblk=nblk, tb=tb, c=c, nlev=nlev),
        grid=(b, nh, 2, nblk),
        in_specs=[
            pl.BlockSpec((1, tb, HG_D), lambda bi, h, ph, jb: (bi, blk_of(ph, jb), h)),
            pl.BlockSpec((1, tb, HG_D), lambda bi, h, ph, jb: (bi, blk_of(ph, jb), nh + nh * ph + h)),
            pl.BlockSpec((1, tb, HG_D), lambda bi, h, ph, jb: (bi, blk_of(ph, jb), 3 * nh + h)),
            pl.BlockSpec((1, tb, HG_D),
                         lambda bi, h, ph, jb: (bi, jnp.where(ph == 0, nblk - 1, nblk - 1 - jb), 4 * nh + h)),
            pl.BlockSpec((1, 1, 8, HG_D), lambda bi, h, ph, jb: (ph, h, 0, 0)),
            pl.BlockSpec((1, HG_D), lambda bi, h, ph, jb: (0, 0)),
            pl.BlockSpec((1,) + msum.shape[1:], lambda bi, h, ph, jb: (ph, 0, 0)),
            pl.BlockSpec((1,) + mask.shape[1:], lambda bi, h, ph, jb: (ph, 0, 0, 0)),
        ],
        out_specs=pl.BlockSpec(
            (1, tb, HG_D), lambda bi, h, ph, jb: (bi, jnp.where(ph == 0, nblk - 1, nblk - 1 - jb), h)),
        out_shape=jax.ShapeDtypeStruct((b, t, HG_WIDTH), BF16),
        scratch_shapes=[pltpu.VMEM((t, HG_D), F32), pltpu.VMEM((HG_D, HG_D), F32)],
        compiler_params=_cparams(("arbitrary",) * 4),
        name="hgrn_scan",
    )(u, u, u, u, lbp, onorm, msum, mask)


def _gla_kernel(q_ref, k_ref, v_ref, gate_ref, r_ref, w2_ref, gb_ref, on_ref, msum_ref, mask_ref, o_ref,
                ofwd_ref, st_ref, *, nblk, tb, c, nlev):
    ph = pl.program_id(2)
    jb = pl.program_id(3)
    blk = jnp.where(ph == 0, jb, nblk - 1 - jb)

    @pl.when(jb == 0)
    def _():
        st_ref[...] = jnp.zeros_like(st_ref)

    w2 = w2_ref[0, 0]
    gb = gb_ref[0, 0]

    def gate_input(rows):
        return _dot(r_ref[0, rows, :].astype(BF16), w2) + gb

    def load_chunk(rows, z):
        q = q_ref[0, rows, :] * (GLA_DK ** -0.5)
        g = _log_sigmoid(z) * (1.0 / GLA_GATE_NORM)
        return q, k_ref[0, rows, :], v_ref[0, rows, :], g

    _scan_both_directions(ph, blk, tb, gate_input, load_chunk, ofwd_ref, o_ref, on_ref, gate_ref,
                          msum_ref=msum_ref, mask_ref=mask_ref, st_ref=st_ref, nch=tb // c, c=c, nlev=nlev)


def _gla_scan(u, r, w2p, gbias, onorm, b, t):
    tb, c = min(SCAN_BLOCK, t), CHUNK
    nblk = t // tb
    msum, mask, nlev = _scan_consts(c)
    nh = GLA_HEADS
    vblk0 = 2 * nh * GLA_DK // GLA_DV

    def blk_of(ph, jb):
        return jnp.where(ph == 0, jb, nblk - 1 - jb)

    return pl.pallas_call(
        functools.partial(_gla_kernel, nblk=nblk, tb=tb, c=c, nlev=nlev),
        grid=(b, nh, 2, nblk),
        in_specs=[
            pl.BlockSpec((1, tb, GLA_DK), lambda bi, h, ph, jb: (bi, blk_of(ph, jb), h)),
            pl.BlockSpec((1, tb, GLA_DK), lambda bi, h, ph, jb: (bi, blk_of(ph, jb), nh + h)),
            pl.BlockSpec((1, tb, GLA_DV), lambda bi, h, ph, jb: (bi, blk_of(ph, jb), vblk0 + h)),
            pl.BlockSpec((1, tb, GLA_DV),
                         lambda bi, h, ph, jb: (bi, jnp.where(ph == 0, nblk - 1, nblk - 1 - jb), vblk0 + nh + h)),
            pl.BlockSpec((1, tb, GLA_R_PAD), lambda bi, h, ph, jb: (bi, blk_of(ph, jb), 0)),
            pl.BlockSpec((1, 1, GLA_R_PAD, GLA_DK), lambda bi, h, ph, jb: (ph, h, 0, 0)),
            pl.BlockSpec((1, 1, 1, GLA_DK), lambda bi, h, ph, jb: (ph, h, 0, 0)),
            pl.BlockSpec((1, GLA_DV), lambda bi, h, ph, jb: (0, 0)),
            pl.BlockSpec((1,) + msum.shape[1:], lambda bi, h, ph, jb: (ph, 0, 0)),
            pl.BlockSpec((1,) + mask.shape[1:], lambda bi, h, ph, jb: (ph, 0, 0, 0)),
        ],
        out_specs=pl.BlockSpec(
            (1, tb, GLA_DV), lambda bi, h, ph, jb: (bi, jnp.where(ph == 0, nblk - 1, nblk - 1 - jb), h)),
        out_shape=jax.ShapeDtypeStruct((b, t, nh * GLA_DV), BF16),
        scratch_shapes=[pltpu.VMEM((t, GLA_DV), F32), pltpu.VMEM((GLA_DV, GLA_DK), F32)],
        compiler_params=_cparams(("arbitrary",) * 4),
        name="gla_scan",
    )(u, u, u, u, r, w2p, gbias, onorm, msum, mask)


def _rope128(x, cos, sin):
    lane = lax.broadcasted_iota(jnp.int32, x.shape, 1)
    half = MLA_ROPE // 2
    swapped = jnp.where(lane < half, pltpu.roll(x, LANE - half, 1), pltpu.roll(x, half, 1))
    return x * cos + swapped * sin


def _mla_prep_kernel(c_ref, cos_ref, sin_ref, qag_ref, kvag_ref, wq_ref, wk_ref, wv_ref, gq_ref, gk_ref,
                     q_ref, k_ref, v_ref):
    c = c_ref[...]
    cq = c[:, :Q_LORA]
    ckv = c[:, Q_LORA:Q_LORA + KV_LORA]
    kpe = c[:, Q_LORA + KV_LORA:]
    cqn = (cq * lax.rsqrt(jnp.mean(cq * cq, axis=-1, keepdims=True) + EPS) * qag_ref[...]).astype(BF16)
    ckvn = (ckv * lax.rsqrt(jnp.mean(ckv * ckv, axis=-1, keepdims=True) + EPS) * kvag_ref[...]).astype(BF16)
    cos = cos_ref[...]
    sin = sin_ref[...]
    gq = gq_ref[...]
    gk = gk_ref[...]
    q_raw = _dot(cqn, wq_ref[...])
    kn_raw = _dot(ckvn, wk_ref[...])
    v_ref[0] = _dot_nt(wv_ref[...], ckvn).astype(BF16)
    kpe_ss = jnp.sum(kpe * kpe, axis=-1, keepdims=True)
    kpe_rot = _rope128(kpe * gk[:, LANE:], cos, sin)
    for h in range(MLA_HEADS):
        lo = h * MLA_QK_PAD
        qn = q_raw[:, lo:lo + LANE]
        qr = q_raw[:, lo + LANE:lo + 2 * LANE]
        ss = jnp.sum(qn * qn, axis=-1, keepdims=True) + jnp.sum(qr * qr, axis=-1, keepdims=True)
        rinv = lax.rsqrt(ss * (1.0 / MLA_QK) + EPS) * ATTN_Q_SCALE
        q_ref[:, lo:lo + LANE] = (qn * rinv * gq[:, :LANE]).astype(BF16)
        q_ref[:, lo + LANE:lo + 2 * LANE] = _rope128(qr * rinv * gq[:, LANE:], cos, sin).astype(BF16)
        kn = kn_raw[:, h * LANE:(h + 1) * LANE]
        ssk = jnp.sum(kn * kn, axis=-1, keepdims=True) + kpe_ss
        rinvk = lax.rsqrt(ssk * (1.0 / MLA_QK) + EPS)
        k_ref[:, lo:lo + LANE] = (kn * rinvk * gk[:, :LANE]).astype(BF16)
        k_ref[:, lo + LANE:lo + 2 * LANE] = (kpe_rot * rinvk).astype(BF16)


def _mla_prep(c, cos, sin, qag, kvag, wq, wk, wv, gq, gk, t):
    n = c.shape[0]
    tm = min(256, t)
    tps = t // tm
    full = lambda a: pl.BlockSpec(a.shape, lambda i: (0,) * a.ndim)
    return pl.pallas_call(
        _mla_prep_kernel,
        grid=(n // tm,),
        in_specs=[
            pl.BlockSpec((tm, MLA_C_PAD), lambda i: (i, 0)),
            pl.BlockSpec((tm, LANE), lambda i: (i % tps, 0)),
            pl.BlockSpec((tm, LANE), lambda i: (i % tps, 0)),
            full(qag), full(kvag), full(wq), full(wk), full(wv), full(gq), full(gk),
        ],
        out_specs=[
            pl.BlockSpec((tm, MLA_HEADS * MLA_QK_PAD), lambda i: (i, 0)),
            pl.BlockSpec((tm, MLA_HEADS * MLA_QK_PAD), lambda i: (i, 0)),
            pl.BlockSpec((1, MLA_HEADS * MLA_V, tm), lambda i: (i // tps, 0, i % tps)),
        ],
        out_shape=[
            jax.ShapeDtypeStruct((n, MLA_HEADS * MLA_QK_PAD), BF16),
            jax.ShapeDtypeStruct((n, MLA_HEADS * MLA_QK_PAD), BF16),
            jax.ShapeDtypeStruct((n // t, MLA_HEADS * MLA_V, t), BF16),
        ],
        compiler_params=_cparams(("arbitrary",)),
        name="mla_prep",
    )(c, cos, sin, qag, kvag, wq, wk, wv, gq, gk)


def _attn_kernel(q_ref, k_ref, vt_ref, o_ref, *, tk, nk, nsplit):
    tq = q_ref.shape[1]
    th = tq // nsplit
    qs = [q_ref[0, i * th:(i + 1) * th, :] for i in range(nsplit)]

    def scores(ci):
        kc = k_ref[0, ci * tk:(ci + 1) * tk, :]
        return [_dot_nt(kc, q) for q in qs]

    def update(ci, s_list, state):
        vt = vt_ref[0, :, ci * tk:(ci + 1) * tk]
        out = []
        for s, (m, l, acc) in zip(s_list, state):
            m_new = jnp.maximum(m, jnp.max(s, axis=0, keepdims=True))
            alpha = jnp.exp2(m - m_new)
            p = jnp.exp2(s - m_new)
            l = alpha * l + jnp.sum(p, axis=0, keepdims=True)
            acc = alpha * acc + _dot(vt, p.astype(BF16))
            out.append((m_new, l, acc))
        return out

    state = [(jnp.full((1, th), -jnp.inf, F32), jnp.zeros((1, th), F32), jnp.zeros((MLA_V, th), F32))
             for _ in range(nsplit)]
    s_cur = scores(0)
    for ci in range(nk):
        s_next = scores(ci + 1) if ci + 1 < nk else None
        state = update(ci, s_cur, state)
        s_cur = s_next
    for i, (_, l, acc) in enumerate(state):
        o_ref[0, i * th:(i + 1) * th, :] = (acc / l).T.astype(BF16)


def _attention(q, k, v, b, t):
    tq, tk = min(ATTN_TQ, t), min(ATTN_TK, t)
    return pl.pallas_call(
        functools.partial(_attn_kernel, tk=tk, nk=t // tk, nsplit=ATTN_SPLIT),
        grid=(b, MLA_HEADS, t // tq),
        in_specs=[
            pl.BlockSpec((1, tq, MLA_QK_PAD), lambda bi, h, i: (bi, i, h)),
            pl.BlockSpec((1, t, MLA_QK_PAD), lambda bi, h, i: (bi, 0, h)),
            pl.BlockSpec((1, MLA_V, t), lambda bi, h, i: (bi, h, 0)),
        ],
        out_specs=pl.BlockSpec((1, tq, MLA_V), lambda bi, h, i: (bi, i, h)),
        out_shape=jax.ShapeDtypeStruct((b, t, MLA_HEADS * MLA_V), BF16),
        compiler_params=_cparams(("arbitrary",) * 3),
        name="mla_attention",
    )(q, k, v)


def _rope_tables(t):
    half = MLA_ROPE // 2
    inv_freq = ROPE_THETA ** (-jnp.arange(half, dtype=F32) / half)
    ang = jnp.arange(t, dtype=jnp.int32).astype(F32)[:, None] * inv_freq[None, :]
    cos, sin = jnp.cos(ang), jnp.sin(ang)
    zeros = jnp.zeros((t, LANE - MLA_ROPE), F32)
    return (jnp.concatenate([cos, cos, zeros], axis=-1), jnp.concatenate([-sin, sin, zeros], axis=-1))


def _pad_cols(a, width):
    return jnp.pad(a, [(0, 0)] * (a.ndim - 1) + [(0, width - a.shape[-1])])


def _tile_cols(w, tn):
    k, n = w.shape[-2:]
    lead = w.shape[:-2]
    w = w.reshape(lead + (k, n // tn, tn))
    return jnp.swapaxes(w, -3, -2)


def _prep_even(e, w):
    ev_in = w['ev_w_in'][e]
    w_hg = _tile_cols(ev_in[:, :5 * HG_WIDTH].astype(BF16), INPROJ_TILE)
    w_mla = _pad_cols(ev_in[:, 5 * HG_WIDTH:], MLA_C_PAD).astype(BF16)
    uq = w['mla_w_uq'][e].reshape(Q_LORA, MLA_HEADS, MLA_QK)
    wq = _pad_cols(uq, MLA_QK_PAD).reshape(Q_LORA, MLA_HEADS * MLA_QK_PAD).astype(BF16)
    ukv = w['mla_w_ukv'][e].reshape(KV_LORA, MLA_HEADS, MLA_NOPE + MLA_V)
    wk = ukv[:, :, :MLA_NOPE].reshape(KV_LORA, MLA_HEADS * MLA_NOPE).astype(BF16)
    wv = ukv[:, :, MLA_NOPE:].reshape(KV_LORA, MLA_HEADS * MLA_V).T.astype(BF16)
    p = jax.nn.softmax(w['hgrn_lb'].astype(F32), axis=1)
    lb = jnp.cumsum(p, axis=1)
    lb = (lb - lb[:, :1])[:, e].reshape(2, HG_HEADS, 1, HG_D)
    lbp = jnp.concatenate([jnp.log(lb), jnp.log1p(-lb), 1.0 - lb, jnp.zeros((2, HG_HEADS, 5, HG_D), F32)], axis=2)
    w_out = w['ev_w_out'][e].astype(BF16)
    return dict(
        w_hg=w_hg, w_mla=w_mla, wq=wq, wk=wk, wv=wv, lbp=lbp,
        hg_onorm=w['hgrn_onorm_g'][e].reshape(1, HG_D),
        qag=w['mla_qa_norm_g'][e].reshape(1, Q_LORA), kvag=w['mla_kva_norm_g'][e].reshape(1, KV_LORA),
        gq=_pad_cols(w['mla_qn_g'][e].reshape(1, MLA_QK), MLA_QK_PAD),
        gk=_pad_cols(w['mla_kn_g'][e].reshape(1, MLA_QK), MLA_QK_PAD),
        w_out_hg=w_out[:HG_WIDTH], w_out_mla=w_out[HG_WIDTH:],
    )


def _prep_odd(e, w):
    od_in = w['od_w_in'][e]
    w_main = _tile_cols(od_in[:, :GLA_MAIN].astype(BF16), INPROJ_TILE)
    w_r = _pad_cols(od_in[:, GLA_MAIN:], GLA_R_PAD).astype(BF16)
    w2 = w['gla_gk_w2'][e].reshape(2, GLA_RANK, GLA_HEADS, GLA_DK).transpose(0, 2, 1, 3)
    w2p = jnp.zeros((2, GLA_HEADS, GLA_R_PAD, GLA_DK), F32)
    w2p = w2p.at[0, :, :GLA_RANK].set(w2[0]).at[1, :, GLA_RANK:2 * GLA_RANK].set(w2[1]).astype(BF16)
    gbias = w['gla_gk_b'][e].reshape(2, GLA_HEADS, 1, GLA_DK)
    return dict(w_main=w_main, w_r=w_r, w2p=w2p, gbias=gbias,
                onorm=w['gla_onorm_g'][e].reshape(1, GLA_DV), w_out=w['od_w_out'][e].astype(BF16))


def _trunk(x3, boff, nb_total, table, norm_g, w13, w2, evens, odds):
    b, t, _ = x3.shape
    n = b * t
    x = x3.reshape(n, D_MODEL)
    mod = _Mod(table, norm_g, nb_total, boff, t // TOKEN_TILE)
    cos, sin = _rope_tables(t)
    for layer in range(DEPTH):
        x = _ffn(x, mod, layer, 0, 0, w13, w2)
        e = layer // 2
        if layer % 2 == 0:
            p = evens[e]
            u_hg, c_mla = _inproj(x, mod, layer, p['w_hg'], p['w_mla'])
            o_hg = _hgrn_scan(u_hg.reshape(b, t, 5 * HG_WIDTH), p['lbp'], p['hg_onorm'], b, t)
            q, k, v = _mla_prep(c_mla, cos, sin, p['qag'], p['kvag'], p['wq'], p['wk'], p['wv'], p['gq'], p['gk'], t)
            o_mla = _attention(q.reshape(b, t, -1), k.reshape(b, t, -1), v, b, t)
            x = _outproj(x, mod, layer, [o_hg.reshape(n, HG_WIDTH), o_mla.reshape(n, MLA_HEADS * MLA_V)],
                         [p['w_out_hg'], p['w_out_mla']])
        else:
            p = odds[e]
            u, r = _inproj(x, mod, layer, p['w_main'], p['w_r'])
            o = _gla_scan(u.reshape(b, t, GLA_MAIN), r.reshape(b, t, GLA_R_PAD), p['w2p'], p['gbias'], p['onorm'], b, t)
            x = _outproj(x, mod, layer, [o.reshape(n, GLA_HEADS * GLA_DV)], [p['w_out']])
        x = _ffn(x, mod, layer, 2, 1, w13, w2)
    return x.reshape(b, t, D_MODEL)


def kernel(x_prompt, x_sample, c_prompt, c_sample, ada_w, ada_b, norm_g, ffn_w13, ffn_w2, ev_w_in, ev_w_out, hgrn_lb, hgrn_onorm_g, mla_qa_norm_g, mla_w_uq, mla_kva_norm_g, mla_w_ukv, mla_qn_g, mla_kn_g, od_w_in, od_w_out, gla_gk_w2, gla_gk_b, gla_onorm_g):
    w = dict(ev_w_in=ev_w_in, ev_w_out=ev_w_out, hgrn_lb=hgrn_lb, hgrn_onorm_g=hgrn_onorm_g,
             mla_qa_norm_g=mla_qa_norm_g, mla_w_uq=mla_w_uq, mla_kva_norm_g=mla_kva_norm_g,
             mla_w_ukv=mla_w_ukv, mla_qn_g=mla_qn_g, mla_kn_g=mla_kn_g, od_w_in=od_w_in,
             od_w_out=od_w_out, gla_gk_w2=gla_gk_w2, gla_gk_b=gla_gk_b, gla_onorm_g=gla_onorm_g)
    bp, bs = x_prompt.shape[0], x_sample.shape[0]
    nb_total = bp + bs
    c_all = jnp.concatenate([c_prompt, c_sample], axis=0)
    mod = _ada_mod(c_all, ada_w, ada_b)
    table = mod.reshape(DEPTH * nb_total * 3 * N_SUB, 1, D_MODEL)
    ng = norm_g.reshape(DEPTH * N_SUB, 1, D_MODEL)
    w13 = _tile_cols(ffn_w13.astype(BF16), FFN_TILE)
    w2 = ffn_w2.astype(BF16)
    evens = [_prep_even(e, w) for e in range((DEPTH + 1) // 2)]
    odds = [_prep_odd(e, w) for e in range(DEPTH // 2)]
    y_prompt = _trunk(x_prompt, 0, nb_total, table, ng, w13, w2, evens, odds)
    y_sample = _trunk(x_sample, bp, nb_total, table, ng, w13, w2, evens, odds)
    return (y_prompt, y_sample)
```

```python
import functools
import math

import numpy as np
import jax
import jax.numpy as jnp
from jax import lax
from jax.experimental import pallas as pl
from jax.experimental.pallas import tpu as pltpu

F32 = jnp.float32
BF16 = jnp.bfloat16

D_MODEL = 2048
DEPTH = 4
N_SUB = 3
EPS = 1e-6
FFN_DIM = 5632

HG_HEADS = 8
HG_D = 128
HG_WIDTH = HG_HEADS * HG_D

MLA_HEADS = 8
MLA_NOPE = 128
MLA_ROPE = 64
MLA_QK = MLA_NOPE + MLA_ROPE
MLA_V = 128
MLA_QK_PAD = 256
Q_LORA = 512
KV_LORA = 512
MLA_C_PAD = 1152
ROPE_THETA = 10000.0

GLA_HEADS = 4
GLA_DK = 256
GLA_DV = 512
GLA_RANK = 16
GLA_GATE_NORM = 16.0
GLA_MAIN = 2 * GLA_HEADS * GLA_DK + 2 * GLA_HEADS * GLA_DV
GLA_R_PAD = 128

CHUNK = 64
LANE = 128
VMEM_LIMIT = 56 * 1024 * 1024

TOKEN_TILE = 512
FFN_TOKEN_TILE = 1024
FFN_TILE = 512
INPROJ_TOKEN_TILE = 1024
INPROJ_TILE = 512
ROW_GROUPS = 2
SCAN_BLOCK = 1024
ATTN_TQ = 512
ATTN_SPLIT = 2
ATTN_TK = 512
LOG2E = math.log2(math.e)
ATTN_Q_SCALE = MLA_QK ** -0.5 * LOG2E


def _cparams(sem):
    return pltpu.CompilerParams(dimension_semantics=sem, vmem_limit_bytes=VMEM_LIMIT)


def _silu(x):
    return x * jax.nn.sigmoid(x)


def _log_sigmoid(z):
    return jnp.minimum(z, 0.0) - jnp.log(1.0 + jnp.exp(-jnp.abs(z)))


def _adaln(x, g, scale, shift):
    ms = jnp.mean(x * x, axis=-1, keepdims=True)
    y = x * lax.rsqrt(ms + EPS) * g
    return y * (1.0 + scale) + shift


def _dot(a, b):
    return jnp.dot(a, b, preferred_element_type=F32)


def _dot_nt(a, b):
    return lax.dot_general(a, b, (((1,), (1,)), ((), ())), preferred_element_type=F32)


def _dot_tn(a, b):
    return lax.dot_general(a, b, (((0,), (0,)), ((), ())), preferred_element_type=F32)


def _ada_kernel(c_ref, w_ref, b_ref, o_ref):
    cond = _silu(c_ref[...]).astype(BF16)
    o_ref[0] = _dot(cond, w_ref[0].astype(BF16)) + b_ref[0]


def _ada_mod(c_all, ada_w, ada_b):
    nb = c_all.shape[0]
    n_out = ada_w.shape[-1]
    tn = 1024
    return pl.pallas_call(
        _ada_kernel,
        grid=(DEPTH, n_out // tn),
        in_specs=[
            pl.BlockSpec((nb, D_MODEL), lambda l, j: (0, 0)),
            pl.BlockSpec((1, D_MODEL, tn), lambda l, j: (l, 0, j)),
            pl.BlockSpec((1, 1, tn), lambda l, j: (l, 0, j)),
        ],
        out_specs=pl.BlockSpec((1, nb, tn), lambda l, j: (l, 0, j)),
        out_shape=jax.ShapeDtypeStruct((DEPTH, nb, n_out), F32),
        compiler_params=_cparams(("arbitrary", "arbitrary")),
        name="ada_mod",
    )(c_all, ada_w, ada_b.reshape(DEPTH, 1, n_out))


class _Mod:
    def __init__(self, table, norm_g, nb_total, boff, seq_len):
        self.table = table
        self.norm_g = norm_g
        self.nb_total = nb_total
        self.boff = boff
        self.seq_len = seq_len

    def spec(self, layer, sub, kind, tile):
        nb, boff, tps = self.nb_total, self.boff, self.seq_len // tile
        return pl.BlockSpec(
            (1, 1, D_MODEL),
            lambda i, *_: (((layer * nb + boff + i // tps) * 9 + sub * 3 + kind), 0, 0))

    def norm_spec(self, layer, sub):
        return pl.BlockSpec((1, 1, D_MODEL), lambda i, *_: (layer * N_SUB + sub, 0, 0))


def _ffn_kernel(x_ref, sh_ref, sc_ref, gt_ref, ng_ref, w1_ref, w3_ref, w2_ref, o_ref, h_ref, *, nj, ngroups):
    acc_ref = o_ref
    j = pl.program_id(1)
    rg = x_ref.shape[0] // ngroups

    def body(first, last):
        w1, w3, w2 = w1_ref[...], w3_ref[...], w2_ref[...]
        ups = []
        for g in range(ngroups):
            rows = slice(g * rg, (g + 1) * rg)
            if first:
                h = _adaln(x_ref[rows, :], ng_ref[0], sc_ref[0], sh_ref[0]).astype(BF16)
                h_ref[rows, :] = h
            else:
                h = h_ref[rows, :]
            ups.append((_dot(h, w1), _dot(h, w3)))
        for g, (a, u) in enumerate(ups):
            rows = slice(g * rg, (g + 1) * rg)
            down = _dot((_silu(a) * u).astype(BF16), w2)
            acc = down if first else acc_ref[rows, :] + down
            if last:
                o_ref[rows, :] = x_ref[rows, :] + (0.5 * gt_ref[0]) * acc
            else:
                acc_ref[rows, :] = acc

    pl.when(j == 0)(lambda: body(True, False))
    pl.when(jnp.logical_and(j > 0, j < nj - 1))(lambda: body(False, False))
    pl.when(j == nj - 1)(lambda: body(False, True))


def _ffn(x, mod, layer, sub, which, w13, w2):
    n = x.shape[0]
    tm, tf = min(FFN_TOKEN_TILE, mod.seq_len), FFN_TILE
    nj = FFN_DIM // tf
    return pl.pallas_call(
        functools.partial(_ffn_kernel, nj=nj, ngroups=ROW_GROUPS),
        grid=(n // tm, nj),
        in_specs=[
            pl.BlockSpec((tm, D_MODEL), lambda i, j: (i, 0)),
            mod.spec(layer, sub, 0, tm), mod.spec(layer, sub, 1, tm), mod.spec(layer, sub, 2, tm),
            mod.norm_spec(layer, sub),
            pl.BlockSpec((None, None, D_MODEL, tf), lambda i, j: (layer, which, 0, j)),
            pl.BlockSpec((None, None, D_MODEL, tf), lambda i, j: (layer, which, 0, j + nj)),
            pl.BlockSpec((None, None, tf, D_MODEL), lambda i, j: (layer, which, j, 0)),
        ],
        out_specs=pl.BlockSpec((tm, D_MODEL), lambda i, j: (i, 0)),
        out_shape=jax.ShapeDtypeStruct((n, D_MODEL), F32),
        scratch_shapes=[pltpu.VMEM((tm, D_MODEL), BF16)],
        compiler_params=_cparams(("arbitrary", "arbitrary")),
        name="ffn",
    )(x, mod.table, mod.table, mod.table, mod.norm_g, w13, w13, w2)


def _inproj_kernel(x_ref, sh_ref, sc_ref, ng_ref, wm_ref, ws_ref, om_ref, os_ref, h_ref, *, ngroups):
    j = pl.program_id(1)
    rg = x_ref.shape[0] // ngroups

    @pl.when(j == 0)
    def _():
        for g in range(ngroups):
            rows = slice(g * rg, (g + 1) * rg)
            h = _adaln(x_ref[rows, :], ng_ref[0], sc_ref[0], sh_ref[0]).astype(BF16)
            h_ref[rows, :] = h
            om_ref[rows, :] = _dot(h, wm_ref[...])
            os_ref[rows, :] = _dot(h, ws_ref[...])

    @pl.when(j > 0)
    def _():
        om_ref[...] = _dot(h_ref[...], wm_ref[...])


def _inproj(x, mod, layer, w_main, w_small):
    n = x.shape[0]
    tm = min(INPROJ_TOKEN_TILE, mod.seq_len)
    ntiles, _, tn = w_main.shape
    n_main, n_small = ntiles * tn, w_small.shape[1]
    return pl.pallas_call(
        functools.partial(_inproj_kernel, ngroups=ROW_GROUPS),
        grid=(n // tm, ntiles),
        in_specs=[
            pl.BlockSpec((tm, D_MODEL), lambda i, j: (i, 0)),
            mod.spec(layer, 1, 0, tm), mod.spec(layer, 1, 1, tm),
            mod.norm_spec(layer, 1),
            pl.BlockSpec((None, D_MODEL, tn), lambda i, j: (j, 0, 0)),
            pl.BlockSpec((D_MODEL, n_small), lambda i, j: (0, 0)),
        ],
        out_specs=[
            pl.BlockSpec((tm, tn), lambda i, j: (i, j)),
            pl.BlockSpec((tm, n_small), lambda i, j: (i, 0)),
        ],
        out_shape=[jax.ShapeDtypeStruct((n, n_main), F32), jax.ShapeDtypeStruct((n, n_small), F32)],
        scratch_shapes=[pltpu.VMEM((tm, D_MODEL), BF16)],
        compiler_params=_cparams(("arbitrary", "arbitrary")),
        name="inproj",
    )(x, mod.table, mod.table, mod.norm_g, w_main, w_small)


def _outproj_kernel(x_ref, gt_ref, *refs):
    o_ref = refs[-1]
    npair = (len(refs) - 1) // 2
    y = _dot(refs[0][...], refs[npair][...])
    for p in range(1, npair):
        y = y + _dot(refs[p][...], refs[npair + p][...])
    o_ref[...] = x_ref[...] + gt_ref[0] * y


def _outproj(x, mod, layer, mixes, ws):
    n = x.shape[0]
    tm = TOKEN_TILE
    in_specs = [pl.BlockSpec((tm, D_MODEL), lambda i: (i, 0)), mod.spec(layer, 1, 2, tm)]
    in_specs += [pl.BlockSpec((tm, m.shape[1]), lambda i: (i, 0)) for m in mixes]
    in_specs += [pl.BlockSpec(w.shape, lambda i: (0, 0)) for w in ws]
    return pl.pallas_call(
        _outproj_kernel,
        grid=(n // tm,),
        in_specs=in_specs,
        out_specs=pl.BlockSpec((tm, D_MODEL), lambda i: (i, 0)),
        out_shape=jax.ShapeDtypeStruct((n, D_MODEL), F32),
        compiler_params=_cparams(("arbitrary",)),
        name="outproj",
    )(x, mod.table, *mixes, *ws)


def _scan_consts(c):
    nlev = int(math.log2(c))
    t = np.arange(c)
    row, col = t[:, None], t[None, :]
    blocks = [col <= row]
    masks = [np.eye(c, dtype=bool)]
    for lev in range(nlev):
        s = 1 << lev
        blk = t // s
        odd = (blk % 2) == 1
        bstart = (blk * s)[:, None]
        bend = bstart + s - 1
        as_query = (col >= bstart) & (col <= row)
        as_key = (col > row) & (col <= bend)
        blocks.append(np.where(odd[:, None], as_query, as_key))
        masks.append(odd[:, None] & (~odd[None, :]) & ((row // (2 * s)) == (col // (2 * s))))
    ones = np.ones((8, c), dtype=bool)
    m_f = np.concatenate(blocks + [ones], axis=0)
    m_b = np.concatenate([b[::-1, ::-1] for b in blocks] + [ones], axis=0)
    msum = jnp.asarray(np.stack([np.tile(m_f, (1, 2)), np.tile(m_b, (1, 2))]).astype(np.float32), dtype=BF16)
    mask = jnp.asarray(np.stack([np.stack(masks), np.stack([m[::-1, ::-1] for m in masks])]).astype(np.float32))
    return msum, mask, nlev


def _split2(g):
    hi = g.astype(BF16)
    lo = (g - hi.astype(F32)).astype(BF16)
    return hi, lo


def _scan_block(backward, gate_input, load_chunk, emit, msum_ref, mask_ref, st_ref, *, nch, c, nlev, state_first):
    msum = msum_ref[0]
    masks = [mask_ref[0, i] > 0.5 for i in range(nlev + 1)]
    starts = [((nch - 1 - ci) if backward else ci) * c for ci in range(nch)]
    order = [slice(r0, r0 + c) for r0 in starts]

    def stage1(rows, z):
        q, k, v, g = load_chunk(rows, z)
        hi, lo = _split2(g * LOG2E)
        return q, k, v.astype(BF16), _dot(msum, jnp.concatenate([hi, lo], axis=0))

    def stage2(q, k, vb, x):
        e = jnp.exp2(x)
        tot = x[(1 + nlev) * c:(1 + nlev) * c + 1]
        dtot = e[(1 + nlev) * c:(1 + nlev) * c + 1]
        qe = (q * e[0:c]).astype(BF16)
        ke = (k * jnp.exp2(tot - x[0:c])).astype(BF16)
        att = jnp.where(masks[0], _dot_nt(q.astype(BF16), k.astype(BF16)), 0.0)
        for lev in range(nlev):
            es = e[(1 + lev) * c:(2 + lev) * c]
            att = jnp.where(masks[1 + lev], _dot_nt((q * es).astype(BF16), (k * es).astype(BF16)), att)
        return qe, dtot, att.astype(BF16), vb, _dot_tn(vb, ke)

    def stage3(rows, st, qe, dtot, att, vb, inc):
        emit(rows, _dot(att, vb) + _dot_nt(qe, st.astype(BF16)))
        return st * dtot + inc

    zs, s1, s2 = {}, {}, {}
    st = st_ref[...]
    for step in range(nch + 3):
        if step < nch:
            zs[step] = gate_input(order[step])
        if 0 <= step - 1 < nch:
            s1[step - 1] = stage1(order[step - 1], zs.pop(step - 1))
        if state_first and 0 <= step - 3 < nch:
            st = stage3(order[step - 3], st, *s2.pop(step - 3))
        if 0 <= step - 2 < nch:
            s2[step - 2] = stage2(*s1.pop(step - 2))
        if not state_first and 0 <= step - 3 < nch:
            st = stage3(order[step - 3], st, *s2.pop(step - 3))
    st_ref[...] = st


def _scan_both_directions(ph, blk, tb, gate_input, load_chunk, ofwd_ref, o_ref, on_ref, gate_ref, **kw):
    c = kw["c"]

    def seq_rows(rows):
        return pl.ds(pl.multiple_of(blk * tb + rows.start, c), c)

    def emit_fwd(rows, o):
        ofwd_ref[seq_rows(rows), :] = o

    def emit_bwd(rows, o):
        o_sum = ofwd_ref[seq_rows(rows), :] + o
        ms = jnp.mean(o_sum * o_sum, axis=-1, keepdims=True)
        o_ref[0, rows, :] = (o_sum * lax.rsqrt(ms + EPS) * on_ref[...] * _silu(gate_ref[0, rows, :])).astype(BF16)

    pl.when(ph == 0)(lambda: _scan_block(False, gate_input, load_chunk, emit_fwd, **kw))
    pl.when(ph == 1)(lambda: _scan_block(True, gate_input, load_chunk, emit_bwd, **kw))


def _hgrn_kernel(q_ref, z_ref, v_ref, gate_ref, lb_ref, on_ref, msum_ref, mask_ref, o_ref, ofwd_ref, st_ref,
                 *, nblk, tb, c, nlev):
    ph = pl.program_id(2)
    jb = pl.program_id(3)
    blk = jnp.where(ph == 0, jb, nblk - 1 - jb)

    @pl.when(jb == 0)
    def _():
        st_ref[...] = jnp.zeros_like(st_ref)

    log_lb = lb_ref[0, 0, 0:1, :]
    log_1mlb = lb_ref[0, 0, 1:2, :]
    one_mlb = lb_ref[0, 0, 2:3, :]

    def gate_input(rows):
        return z_ref[0, rows, :]

    def load_chunk(rows, z):
        q = _silu(q_ref[0, rows, :])
        t = jnp.exp(-jnp.abs(z))
        d = 1.0 + t
        a = log_lb
        b = log_1mlb + (jnp.minimum(z, 0.0) - jnp.log(d))
        g = jnp.maximum(a, b) + jnp.log(1.0 + jnp.exp(-jnp.abs(a - b)))
        k = one_mlb * (jnp.where(z > 0.0, t, 1.0) / d)
        return q, k, v_ref[0, rows, :], g

    _scan_both_directions(ph, blk, tb, gate_input, load_chunk, ofwd_ref, o_ref, on_ref, gate_ref,
                          msum_ref=msum_ref, mask_ref=mask_ref, st_ref=st_ref, nch=tb // c, c=c, nlev=nlev,
                          state_first=False)


def _hgrn_scan(u, lbp, onorm, b, t):
    tb, c = min(SCAN_BLOCK, t), CHUNK
    nblk = t // tb
    msum, mask, nlev = _scan_consts(c)
    nh = HG_HEADS

    def blk_of(ph, jb):
        return jnp.where(ph == 0, jb, nblk - 1 - jb)

    return pl.pallas_call(
        functools.partial(_hgrn_kernel, nblk=nblk, tb=tb, c=c, nlev=nlev),
        grid=(b, nh, 2, nblk),
        in_specs=[
            pl.BlockSpec((1, tb, HG_D), lambda bi, h, ph, jb: (bi, blk_of(ph, jb), h)),
            pl.BlockSpec((1, tb, HG_D), lambda bi, h, ph, jb: (bi, blk_of(ph, jb), nh + nh * ph + h)),
            pl.BlockSpec((1, tb, HG_D), lambda bi, h, ph, jb: (bi, blk_of(ph, jb), 3 * nh + h)),
            pl.BlockSpec((1, tb, HG_D),
                         lambda bi, h, ph, jb: (bi, jnp.where(ph == 0, nblk - 1, nblk - 1 - jb), 4 * nh + h)),
            pl.BlockSpec((1, 1, 8, HG_D), lambda bi, h, ph, jb: (ph, h, 0, 0)),
            pl.BlockSpec((1, HG_D), lambda bi, h, ph, jb: (0, 0)),
            pl.BlockSpec((1,) + msum.shape[1:], lambda bi, h, ph, jb: (ph, 0, 0)),
            pl.BlockSpec((1,) + mask.shape[1:], lambda bi, h, ph, jb: (ph, 0, 0, 0)),
        ],
        out_specs=pl.BlockSpec(
            (1, tb, HG_D), lambda bi, h, ph, jb: (bi, jnp.where(ph == 0, nblk - 1, nblk - 1 - jb), h)),
        out_shape=jax.ShapeDtypeStruct((b, t, HG_WIDTH), BF16),
        scratch_shapes=[pltpu.VMEM((t, HG_D), F32), pltpu.VMEM((HG_D, HG_D), F32)],
        compiler_params=_cparams(("arbitrary",) * 4),
        name="hgrn_scan",
    )(u, u, u, u, lbp, onorm, msum, mask)


def _gla_kernel(q_ref, k_ref, v_ref, gate_ref, r_ref, w2_ref, gb_ref, on_ref, msum_ref, mask_ref, o_ref,
                ofwd_ref, st_ref, *, nblk, tb, c, nlev):
    ph = pl.program_id(2)
    jb = pl.program_id(3)
    blk = jnp.where(ph == 0, jb, nblk - 1 - jb)

    @pl.when(jb == 0)
    def _():
        st_ref[...] = jnp.zeros_like(st_ref)

    w2 = w2_ref[0, 0]
    gb = gb_ref[0, 0]

    def gate_input(rows):
        return _dot(r_ref[0, rows, :].astype(BF16), w2) + gb

    def load_chunk(rows, z):
        q = q_ref[0, rows, :] * (GLA_DK ** -0.5)
        g = _log_sigmoid(z) * (1.0 / GLA_GATE_NORM)
        return q, k_ref[0, rows, :], v_ref[0, rows, :], g

    _scan_both_directions(ph, blk, tb, gate_input, load_chunk, ofwd_ref, o_ref, on_ref, gate_ref,
                          msum_ref=msum_ref, mask_ref=mask_ref, st_ref=st_ref, nch=tb // c, c=c, nlev=nlev,
                          state_first=True)


def _gla_scan(u, r, w2p, gbias, onorm, b, t):
    tb, c = min(SCAN_BLOCK, t), CHUNK
    nblk = t // tb
    msum, mask, nlev = _scan_consts(c)
    nh = GLA_HEADS
    vblk0 = 2 * nh * GLA_DK // GLA_DV

    def blk_of(ph, jb):
        return jnp.where(ph == 0, jb, nblk - 1 - jb)

    return pl.pallas_call(
        functools.partial(_gla_kernel, nblk=nblk, tb=tb, c=c, nlev=nlev),
        grid=(b, nh, 2, nblk),
        in_specs=[
            pl.BlockSpec((1, tb, GLA_DK), lambda bi, h, ph, jb: (bi, blk_of(ph, jb), h)),
            pl.BlockSpec((1, tb, GLA_DK), lambda bi, h, ph, jb: (bi, blk_of(ph, jb), nh + h)),
            pl.BlockSpec((1, tb, GLA_DV), lambda bi, h, ph, jb: (bi, blk_of(ph, jb), vblk0 + h)),
            pl.BlockSpec((1, tb, GLA_DV),
                         lambda bi, h, ph, jb: (bi, jnp.where(ph == 0, nblk - 1, nblk - 1 - jb), vblk0 + nh + h)),
            pl.BlockSpec((1, tb, GLA_R_PAD), lambda bi, h, ph, jb: (bi, blk_of(ph, jb), 0)),
            pl.BlockSpec((1, 1, GLA_R_PAD, GLA_DK), lambda bi, h, ph, jb: (ph, h, 0, 0)),
            pl.BlockSpec((1, 1, 1, GLA_DK), lambda bi, h, ph, jb: (ph, h, 0, 0)),
            pl.BlockSpec((1, GLA_DV), lambda bi, h, ph, jb: (0, 0)),
            pl.BlockSpec((1,) + msum.shape[1:], lambda bi, h, ph, jb: (ph, 0, 0)),
            pl.BlockSpec((1,) + mask.shape[1:], lambda bi, h, ph, jb: (ph, 0, 0, 0)),
        ],
        out_specs=pl.BlockSpec(
            (1, tb, GLA_DV), lambda bi, h, ph, jb: (bi, jnp.where(ph == 0, nblk - 1, nblk - 1 - jb), h)),
        out_shape=jax.ShapeDtypeStruct((b, t, nh * GLA_DV), BF16),
        scratch_shapes=[pltpu.VMEM((t, GLA_DV), F32), pltpu.VMEM((GLA_DV, GLA_DK), F32)],
        compiler_params=_cparams(("arbitrary",) * 4),
        name="gla_scan",
    )(u, u, u, u, r, w2p, gbias, onorm, msum, mask)


def _rope128(x, cos, sin):
    lane = lax.broadcasted_iota(jnp.int32, x.shape, 1)
    half = MLA_ROPE // 2
    swapped = jnp.where(lane < half, pltpu.roll(x, LANE - half, 1), pltpu.roll(x, half, 1))
    return x * cos + swapped * sin


def _mla_prep_kernel(c_ref, cos_ref, sin_ref, qag_ref, kvag_ref, wq_ref, wk_ref, wv_ref, gq_ref, gk_ref,
                     q_ref, k_ref, v_ref):
    c = c_ref[...]
    cq = c[:, :Q_LORA]
    ckv = c[:, Q_LORA:Q_LORA + KV_LORA]
    kpe = c[:, Q_LORA + KV_LORA:]
    cqn = (cq * lax.rsqrt(jnp.mean(cq * cq, axis=-1, keepdims=True) + EPS) * qag_ref[...]).astype(BF16)
    ckvn = (ckv * lax.rsqrt(jnp.mean(ckv * ckv, axis=-1, keepdims=True) + EPS) * kvag_ref[...]).astype(BF16)
    cos = cos_ref[...]
    sin = sin_ref[...]
    gq = gq_ref[...]
    gk = gk_ref[...]
    q_raw = _dot(cqn, wq_ref[...])
    kn_raw = _dot(ckvn, wk_ref[...])
    v_ref[...] = _dot(ckvn, wv_ref[...]).astype(BF16)
    kpe_ss = jnp.sum(kpe * kpe, axis=-1, keepdims=True)
    kpe_rot = _rope128(kpe * gk[:, LANE:], cos, sin)
    for h in range(MLA_HEADS):
        lo = h * MLA_QK_PAD
        qn = q_raw[:, lo:lo + LANE]
        qr = q_raw[:, lo + LANE:lo + 2 * LANE]
        ss = jnp.sum(qn * qn, axis=-1, keepdims=True) + jnp.sum(qr * qr, axis=-1, keepdims=True)
        rinv = lax.rsqrt(ss * (1.0 / MLA_QK) + EPS) * ATTN_Q_SCALE
        q_ref[:, lo:lo + LANE] = (qn * rinv * gq[:, :LANE]).astype(BF16)
        q_ref[:, lo + LANE:lo + 2 * LANE] = _rope128(qr * rinv * gq[:, LANE:], cos, sin).astype(BF16)
        kn = kn_raw[:, h * LANE:(h + 1) * LANE]
        ssk = jnp.sum(kn * kn, axis=-1, keepdims=True) + kpe_ss
        rinvk = lax.rsqrt(ssk * (1.0 / MLA_QK) + EPS)
        k_ref[:, lo:lo + LANE] = (kn * rinvk * gk[:, :LANE]).astype(BF16)
        k_ref[:, lo + LANE:lo + 2 * LANE] = (kpe_rot * rinvk).astype(BF16)


def _mla_prep(c, cos, sin, qag, kvag, wq, wk, wv, gq, gk, t):
    n = c.shape[0]
    tm = min(256, t)
    tps = t // tm
    full = lambda a: pl.BlockSpec(a.shape, lambda i: (0,) * a.ndim)
    return pl.pallas_call(
        _mla_prep_kernel,
        grid=(n // tm,),
        in_specs=[
            pl.BlockSpec((tm, MLA_C_PAD), lambda i: (i, 0)),
            pl.BlockSpec((tm, LANE), lambda i: (i % tps, 0)),
            pl.BlockSpec((tm, LANE), lambda i: (i % tps, 0)),
            full(qag), full(kvag), full(wq), full(wk), full(wv), full(gq), full(gk),
        ],
        out_specs=[
            pl.BlockSpec((tm, MLA_HEADS * MLA_QK_PAD), lambda i: (i, 0)),
            pl.BlockSpec((tm, MLA_HEADS * MLA_QK_PAD), lambda i: (i, 0)),
            pl.BlockSpec((tm, MLA_HEADS * MLA_V), lambda i: (i, 0)),
        ],
        out_shape=[
            jax.ShapeDtypeStruct((n, MLA_HEADS * MLA_QK_PAD), BF16),
            jax.ShapeDtypeStruct((n, MLA_HEADS * MLA_QK_PAD), BF16),
            jax.ShapeDtypeStruct((n, MLA_HEADS * MLA_V), BF16),
        ],
        compiler_params=_cparams(("arbitrary",)),
        name="mla_prep",
    )(c, cos, sin, qag, kvag, wq, wk, wv, gq, gk)


def _attn_kernel(q_ref, k_ref, v_ref, o_ref, *, tk, nk, nsplit):
    tq = q_ref.shape[1]
    th = tq // nsplit
    qs = [q_ref[0, i * th:(i + 1) * th, :] for i in range(nsplit)]

    def scores(ci):
        kc = k_ref[0, ci * tk:(ci + 1) * tk, :]
        return [_dot_nt(q, kc) for q in qs]

    def update(ci, s_list, state):
        vc = v_ref[0, ci * tk:(ci + 1) * tk, :]
        out = []
        for s, (m, l, acc) in zip(s_list, state):
            m_new = jnp.maximum(m, jnp.max(s, axis=-1, keepdims=True))
            alpha = jnp.exp2(m - m_new)
            p = jnp.exp2(s - m_new)
            l = alpha * l + jnp.sum(p, axis=-1, keepdims=True)
            acc = alpha * acc + _dot(p.astype(BF16), vc)
            out.append((m_new, l, acc))
        return out

    state = [(jnp.full((th, 1), -jnp.inf, F32), jnp.zeros((th, 1), F32), jnp.zeros((th, MLA_V), F32))
             for _ in range(nsplit)]
    s_cur = scores(0)
    for ci in range(nk):
        s_next = scores(ci + 1) if ci + 1 < nk else None
        state = update(ci, s_cur, state)
        s_cur = s_next
    for i, (_, l, acc) in enumerate(state):
        o_ref[0, i * th:(i + 1) * th, :] = (acc / l).astype(BF16)


def _attention(q, k, v, b, t):
    tq, tk = min(ATTN_TQ, t), min(ATTN_TK, t)
    return pl.pallas_call(
        functools.partial(_attn_kernel, tk=tk, nk=t // tk, nsplit=ATTN_SPLIT),
        grid=(b, MLA_HEADS, t // tq),
        in_specs=[
            pl.BlockSpec((1, tq, MLA_QK_PAD), lambda bi, h, i: (bi, i, h)),
            pl.BlockSpec((1, t, MLA_QK_PAD), lambda bi, h, i: (bi, 0, h)),
            pl.BlockSpec((1, t, MLA_V), lambda bi, h, i: (bi, 0, h)),
        ],
        out_specs=pl.BlockSpec((1, tq, MLA_V), lambda bi, h, i: (bi, i, h)),
        out_shape=jax.ShapeDtypeStruct((b, t, MLA_HEADS * MLA_V), BF16),
        compiler_params=_cparams(("arbitrary",) * 3),
        name="mla_attention",
    )(q, k, v)


def _rope_tables(t):
    half = MLA_ROPE // 2
    inv_freq = ROPE_THETA ** (-jnp.arange(half, dtype=F32) / half)
    ang = jnp.arange(t, dtype=jnp.int32).astype(F32)[:, None] * inv_freq[None, :]
    cos, sin = jnp.cos(ang), jnp.sin(ang)
    zeros = jnp.zeros((t, LANE - MLA_ROPE), F32)
    return (jnp.concatenate([cos, cos, zeros], axis=-1), jnp.concatenate([-sin, sin, zeros], axis=-1))


def _pad_cols(a, width):
    return jnp.pad(a, [(0, 0)] * (a.ndim - 1) + [(0, width - a.shape[-1])])


def _tile_cols(w, tn):
    k, n = w.shape[-2:]
    lead = w.shape[:-2]
    w = w.reshape(lead + (k, n // tn, tn))
    return jnp.swapaxes(w, -3, -2)


def _prep_even(e, w):
    ev_in = w['ev_w_in'][e]
    w_hg = _tile_cols(ev_in[:, :5 * HG_WIDTH].astype(BF16), INPROJ_TILE)
    w_mla = _pad_cols(ev_in[:, 5 * HG_WIDTH:], MLA_C_PAD).astype(BF16)
    uq = w['mla_w_uq'][e].reshape(Q_LORA, MLA_HEADS, MLA_QK)
    wq = _pad_cols(uq, MLA_QK_PAD).reshape(Q_LORA, MLA_HEADS * MLA_QK_PAD).astype(BF16)
    ukv = w['mla_w_ukv'][e].reshape(KV_LORA, MLA_HEADS, MLA_NOPE + MLA_V)
    wk = ukv[:, :, :MLA_NOPE].reshape(KV_LORA, MLA_HEADS * MLA_NOPE).astype(BF16)
    wv = ukv[:, :, MLA_NOPE:].reshape(KV_LORA, MLA_HEADS * MLA_V).astype(BF16)
    p = jax.nn.softmax(w['hgrn_lb'].astype(F32), axis=1)
    lb = jnp.cumsum(p, axis=1)
    lb = (lb - lb[:, :1])[:, e].reshape(2, HG_HEADS, 1, HG_D)
    lbp = jnp.concatenate([jnp.log(lb), jnp.log1p(-lb), 1.0 - lb, jnp.zeros((2, HG_HEADS, 5, HG_D), F32)], axis=2)
    w_out = w['ev_w_out'][e].astype(BF16)
    return dict(
        w_hg=w_hg, w_mla=w_mla, wq=wq, wk=wk, wv=wv, lbp=lbp,
        hg_onorm=w['hgrn_onorm_g'][e].reshape(1, HG_D),
        qag=w['mla_qa_norm_g'][e].reshape(1, Q_LORA), kvag=w['mla_kva_norm_g'][e].reshape(1, KV_LORA),
        gq=_pad_cols(w['mla_qn_g'][e].reshape(1, MLA_QK), MLA_QK_PAD),
        gk=_pad_cols(w['mla_kn_g'][e].reshape(1, MLA_QK), MLA_QK_PAD),
        w_out_hg=w_out[:HG_WIDTH], w_out_mla=w_out[HG_WIDTH:],
    )


def _prep_odd(e, w):
    od_in = w['od_w_in'][e]
    w_main = _tile_cols(od_in[:, :GLA_MAIN].astype(BF16), INPROJ_TILE)
    w_r = _pad_cols(od_in[:, GLA_MAIN:], GLA_R_PAD).astype(BF16)
    w2 = w['gla_gk_w2'][e].reshape(2, GLA_RANK, GLA_HEADS, GLA_DK).transpose(0, 2, 1, 3)
    w2p = jnp.zeros((2, GLA_HEADS, GLA_R_PAD, GLA_DK), F32)
    w2p = w2p.at[0, :, :GLA_RANK].set(w2[0]).at[1, :, GLA_RANK:2 * GLA_RANK].set(w2[1]).astype(BF16)
    gbias = w['gla_gk_b'][e].reshape(2, GLA_HEADS, 1, GLA_DK)
    return dict(w_main=w_main, w_r=w_r, w2p=w2p, gbias=gbias,
                onorm=w['gla_onorm_g'][e].reshape(1, GLA_DV), w_out=w['od_w_out'][e].astype(BF16))


def _trunk(x3, boff, nb_total, table, norm_g, w13, w2, evens, odds):
    b, t, _ = x3.shape
    n = b * t
    x = x3.reshape(n, D_MODEL)
    mod = _Mod(table, norm_g, nb_total, boff, t)
    cos, sin = _rope_tables(t)
    for layer in range(DEPTH):
        x = _ffn(x, mod, layer, 0, 0, w13, w2)
        e = layer // 2
        if layer % 2 == 0:
            p = evens[e]
            u_hg, c_mla = _inproj(x, mod, layer, p['w_hg'], p['w_mla'])
            o_hg = _hgrn_scan(u_hg.reshape(b, t, 5 * HG_WIDTH), p['lbp'], p['hg_onorm'], b, t)
            q, k, v = _mla_prep(c_mla, cos, sin, p['qag'], p['kvag'], p['wq'], p['wk'], p['wv'], p['gq'], p['gk'], t)
            o_mla = _attention(q.reshape(b, t, -1), k.reshape(b, t, -1), v.reshape(b, t, -1), b, t)
            x = _outproj(x, mod, layer, [o_hg.reshape(n, HG_WIDTH), o_mla.reshape(n, MLA_HEADS * MLA_V)],
                         [p['w_out_hg'], p['w_out_mla']])
        else:
            p = odds[e]
            u, r = _inproj(x, mod, layer, p['w_main'], p['w_r'])
            o = _gla_scan(u.reshape(b, t, GLA_MAIN), r.reshape(b, t, GLA_R_PAD), p['w2p'], p['gbias'], p['onorm'], b, t)
            x = _outproj(x, mod, layer, [o.reshape(n, GLA_HEADS * GLA_DV)], [p['w_out']])
        x = _ffn(x, mod, layer, 2, 1, w13, w2)
    return x.reshape(b, t, D_MODEL)


def kernel(x_prompt, x_sample, c_prompt, c_sample, ada_w, ada_b, norm_g, ffn_w13, ffn_w2, ev_w_in, ev_w_out, hgrn_lb, hgrn_onorm_g, mla_qa_norm_g, mla_w_uq, mla_kva_norm_g, mla_w_ukv, mla_qn_g, mla_kn_g, od_w_in, od_w_out, gla_gk_w2, gla_gk_b, gla_onorm_g):
    w = dict(ev_w_in=ev_w_in, ev_w_out=ev_w_out, hgrn_lb=hgrn_lb, hgrn_onorm_g=hgrn_onorm_g,
             mla_qa_norm_g=mla_qa_norm_g, mla_w_uq=mla_w_uq, mla_kva_norm_g=mla_kva_norm_g,
             mla_w_ukv=mla_w_ukv, mla_qn_g=mla_qn_g, mla_kn_g=mla_kn_g, od_w_in=od_w_in,
             od_w_out=od_w_out, gla_gk_w2=gla_gk_w2, gla_gk_b=gla_gk_b, gla_onorm_g=gla_onorm_g)
    bp, bs = x_prompt.shape[0], x_sample.shape[0]
    nb_total = bp + bs
    c_all = jnp.concatenate([c_prompt, c_sample], axis=0)
    mod = _ada_mod(c_all, ada_w, ada_b)
    table = mod.reshape(DEPTH * nb_total * 3 * N_SUB, 1, D_MODEL)
    ng = norm_g.reshape(DEPTH * N_SUB, 1, D_MODEL)
    w13 = ffn_w13.astype(BF16)
    w2 = ffn_w2.astype(BF16)
    evens = [_prep_even(e, w) for e in range((DEPTH + 1) // 2)]
    odds = [_prep_odd(e, w) for e in range(DEPTH // 2)]
    y_prompt = _trunk(x_prompt, 0, nb_total, table, ng, w13, w2, evens, odds)
    y_sample = _trunk(x_sample, bp, nb_total, table, ng, w13, w2, evens, odds)
    return (y_prompt, y_sample)
```

```python
import functools
import math

import numpy as np
import jax
import jax.numpy as jnp
from jax import lax
from jax.experimental import pallas as pl
from jax.experimental.pallas import tpu as pltpu

F32 = jnp.float32
BF16 = jnp.bfloat16

D_MODEL = 2048
DEPTH = 4
N_SUB = 3
EPS = 1e-6
FFN_DIM = 5632

HG_HEADS = 8
HG_D = 128
HG_WIDTH = HG_HEADS * HG_D

MLA_HEADS = 8
MLA_NOPE = 128
MLA_ROPE = 64
MLA_QK = MLA_NOPE + MLA_ROPE
MLA_V = 128
MLA_QK_PAD = 256
Q_LORA = 512
KV_LORA = 512
MLA_C_PAD = 1152
ROPE_THETA = 10000.0

GLA_HEADS = 4
GLA_DK = 256
GLA_DV = 512
GLA_RANK = 16
GLA_GATE_NORM = 16.0
GLA_MAIN = 2 * GLA_HEADS * GLA_DK + 2 * GLA_HEADS * GLA_DV
GLA_R_PAD = 128

CHUNK = 64
LANE = 128
VMEM_LIMIT = 56 * 1024 * 1024

TOKEN_TILE = 512
FFN_TOKEN_TILE = 1024
FFN_TILE = 512
INPROJ_TOKEN_TILE = 1024
INPROJ_TILE = 512
ROW_GROUPS = 2
SCAN_BLOCK = 1024
ATTN_TQ = 512
ATTN_SPLIT = 2
ATTN_TK = 512
LOG2E = math.log2(math.e)
ATTN_Q_SCALE = MLA_QK ** -0.5 * LOG2E


def _cparams(sem):
    return pltpu.CompilerParams(dimension_semantics=sem, vmem_limit_bytes=VMEM_LIMIT)


def _silu(x):
    return x * jax.nn.sigmoid(x)


def _log_sigmoid(z):
    return jnp.minimum(z, 0.0) - jnp.log(1.0 + jnp.exp(-jnp.abs(z)))


def _adaln(x, g, scale, shift):
    ms = jnp.mean(x * x, axis=-1, keepdims=True)
    y = x * lax.rsqrt(ms + EPS) * g
    return y * (1.0 + scale) + shift


def _dot(a, b):
    return jnp.dot(a, b, preferred_element_type=F32)


def _dot_nt(a, b):
    return lax.dot_general(a, b, (((1,), (1,)), ((), ())), preferred_element_type=F32)


def _dot_tn(a, b):
    return lax.dot_general(a, b, (((0,), (0,)), ((), ())), preferred_element_type=F32)


def _ada_kernel(c_ref, w_ref, b_ref, o_ref):
    cond = _silu(c_ref[...]).astype(BF16)
    o_ref[0] = _dot(cond, w_ref[0].astype(BF16)) + b_ref[0]


def _ada_mod(c_all, ada_w, ada_b):
    nb = c_all.shape[0]
    n_out = ada_w.shape[-1]
    tn = 1024
    return pl.pallas_call(
        _ada_kernel,
        grid=(DEPTH, n_out // tn),
        in_specs=[
            pl.BlockSpec((nb, D_MODEL), lambda l, j: (0, 0)),
            pl.BlockSpec((1, D_MODEL, tn), lambda l, j: (l, 0, j)),
            pl.BlockSpec((1, 1, tn), lambda l, j: (l, 0, j)),
        ],
        out_specs=pl.BlockSpec((1, nb, tn), lambda l, j: (l, 0, j)),
        out_shape=jax.ShapeDtypeStruct((DEPTH, nb, n_out), F32),
        compiler_params=_cparams(("arbitrary", "arbitrary")),
        name="ada_mod",
    )(c_all, ada_w, ada_b.reshape(DEPTH, 1, n_out))


class _Mod:
    def __init__(self, table, norm_g, nb_total, boff, seq_len):
        self.table = table
        self.norm_g = norm_g
        self.nb_total = nb_total
        self.boff = boff
        self.seq_len = seq_len

    def spec(self, layer, sub, kind, tile):
        nb, boff, tps = self.nb_total, self.boff, self.seq_len // tile
        return pl.BlockSpec(
            (1, 1, D_MODEL),
            lambda i, *_: (((layer * nb + boff + i // tps) * 9 + sub * 3 + kind), 0, 0))

    def norm_spec(self, layer, sub):
        return pl.BlockSpec((1, 1, D_MODEL), lambda i, *_: (layer * N_SUB + sub, 0, 0))


def _ffn_kernel(x_ref, sh_ref, sc_ref, gt_ref, ng_ref, w1_ref, w3_ref, w2_ref, o_ref, h_ref, *, nj, ngroups):
    acc_ref = o_ref
    j = pl.program_id(1)
    rg = x_ref.shape[0] // ngroups

    def body(first, last):
        w1, w3, w2 = w1_ref[...], w3_ref[...], w2_ref[...]
        ups = []
        for g in range(ngroups):
            rows = slice(g * rg, (g + 1) * rg)
            if first:
                h = _adaln(x_ref[rows, :], ng_ref[0], sc_ref[0], sh_ref[0]).astype(BF16)
                h_ref[rows, :] = h
            else:
                h = h_ref[rows, :]
            ups.append((_dot(h, w1), _dot(h, w3)))
        for g, (a, u) in enumerate(ups):
            rows = slice(g * rg, (g + 1) * rg)
            down = _dot((_silu(a) * u).astype(BF16), w2)
            acc = down if first else acc_ref[rows, :] + down
            if last:
                o_ref[rows, :] = x_ref[rows, :] + (0.5 * gt_ref[0]) * acc
            else:
                acc_ref[rows, :] = acc

    pl.when(j == 0)(lambda: body(True, False))
    pl.when(jnp.logical_and(j > 0, j < nj - 1))(lambda: body(False, False))
    pl.when(j == nj - 1)(lambda: body(False, True))


def _ffn(x, mod, layer, sub, which, w13, w2):
    n = x.shape[0]
    tm, tf = min(FFN_TOKEN_TILE, mod.seq_len), FFN_TILE
    nj = FFN_DIM // tf
    return pl.pallas_call(
        functools.partial(_ffn_kernel, nj=nj, ngroups=ROW_GROUPS),
        grid=(n // tm, nj),
        in_specs=[
            pl.BlockSpec((tm, D_MODEL), lambda i, j: (i, 0)),
            mod.spec(layer, sub, 0, tm), mod.spec(layer, sub, 1, tm), mod.spec(layer, sub, 2, tm),
            mod.norm_spec(layer, sub),
            pl.BlockSpec((None, None, D_MODEL, tf), lambda i, j: (layer, which, 0, j)),
            pl.BlockSpec((None, None, D_MODEL, tf), lambda i, j: (layer, which, 0, j + nj)),
            pl.BlockSpec((None, None, tf, D_MODEL), lambda i, j: (layer, which, j, 0)),
        ],
        out_specs=pl.BlockSpec((tm, D_MODEL), lambda i, j: (i, 0)),
        out_shape=jax.ShapeDtypeStruct((n, D_MODEL), F32),
        scratch_shapes=[pltpu.VMEM((tm, D_MODEL), BF16)],
        compiler_params=_cparams(("arbitrary", "arbitrary")),
        name="ffn",
    )(x, mod.table, mod.table, mod.table, mod.norm_g, w13, w13, w2)


def _inproj_kernel(x_ref, sh_ref, sc_ref, ng_ref, wm_ref, ws_ref, om_ref, os_ref, h_ref, *, ngroups):
    j = pl.program_id(1)
    rg = x_ref.shape[0] // ngroups

    def store_main(rows, res):
        for s in range(om_ref.shape[0]):
            om_ref[s, rows, :] = res[:, s * LANE:(s + 1) * LANE]

    @pl.when(j == 0)
    def _():
        for g in range(ngroups):
            rows = slice(g * rg, (g + 1) * rg)
            h = _adaln(x_ref[rows, :], ng_ref[0], sc_ref[0], sh_ref[0]).astype(BF16)
            h_ref[rows, :] = h
            store_main(rows, _dot(h, wm_ref[...]))
            os_ref[rows, :] = _dot(h, ws_ref[...])

    @pl.when(j > 0)
    def _():
        store_main(slice(None), _dot(h_ref[...], wm_ref[...]))


def _inproj(x, mod, layer, w_main, w_small):
    n = x.shape[0]
    tm = min(INPROJ_TOKEN_TILE, mod.seq_len)
    ntiles, _, tn = w_main.shape
    n_main, n_small = ntiles * tn, w_small.shape[1]
    return pl.pallas_call(
        functools.partial(_inproj_kernel, ngroups=ROW_GROUPS),
        grid=(n // tm, ntiles),
        in_specs=[
            pl.BlockSpec((tm, D_MODEL), lambda i, j: (i, 0)),
            mod.spec(layer, 1, 0, tm), mod.spec(layer, 1, 1, tm),
            mod.norm_spec(layer, 1),
            pl.BlockSpec((None, D_MODEL, tn), lambda i, j: (j, 0, 0)),
            pl.BlockSpec((D_MODEL, n_small), lambda i, j: (0, 0)),
        ],
        out_specs=[
            pl.BlockSpec((tn // LANE, tm, LANE), lambda i, j: (j, i, 0)),
            pl.BlockSpec((tm, n_small), lambda i, j: (i, 0)),
        ],
        out_shape=[jax.ShapeDtypeStruct((n_main // LANE, n, LANE), F32), jax.ShapeDtypeStruct((n, n_small), F32)],
        scratch_shapes=[pltpu.VMEM((tm, D_MODEL), BF16)],
        compiler_params=_cparams(("arbitrary", "arbitrary")),
        name="inproj",
    )(x, mod.table, mod.table, mod.norm_g, w_main, w_small)


def _outproj_kernel(x_ref, gt_ref, *refs):
    o_ref = refs[-1]
    npair = (len(refs) - 1) // 2
    y = _dot(refs[0][...], refs[npair][...])
    for p in range(1, npair):
        y = y + _dot(refs[p][...], refs[npair + p][...])
    o_ref[...] = x_ref[...] + gt_ref[0] * y


def _outproj(x, mod, layer, mixes, ws):
    n = x.shape[0]
    tm = TOKEN_TILE
    in_specs = [pl.BlockSpec((tm, D_MODEL), lambda i: (i, 0)), mod.spec(layer, 1, 2, tm)]
    in_specs += [pl.BlockSpec((tm, m.shape[1]), lambda i: (i, 0)) for m in mixes]
    in_specs += [pl.BlockSpec(w.shape, lambda i: (0, 0)) for w in ws]
    return pl.pallas_call(
        _outproj_kernel,
        grid=(n // tm,),
        in_specs=in_specs,
        out_specs=pl.BlockSpec((tm, D_MODEL), lambda i: (i, 0)),
        out_shape=jax.ShapeDtypeStruct((n, D_MODEL), F32),
        compiler_params=_cparams(("arbitrary",)),
        name="outproj",
    )(x, mod.table, *mixes, *ws)


def _scan_consts(c):
    nlev = int(math.log2(c))
    t = np.arange(c)
    row, col = t[:, None], t[None, :]
    blocks = [col <= row]
    masks = [np.eye(c, dtype=bool)]
    for lev in range(nlev):
        s = 1 << lev
        blk = t // s
        odd = (blk % 2) == 1
        bstart = (blk * s)[:, None]
        bend = bstart + s - 1
        as_query = (col >= bstart) & (col <= row)
        as_key = (col > row) & (col <= bend)
        blocks.append(np.where(odd[:, None], as_query, as_key))
        masks.append(odd[:, None] & (~odd[None, :]) & ((row // (2 * s)) == (col // (2 * s))))
    ones = np.ones((8, c), dtype=bool)
    m_f = np.concatenate(blocks + [ones], axis=0)
    m_b = np.concatenate([b[::-1, ::-1] for b in blocks] + [ones], axis=0)
    msum = jnp.asarray(np.stack([np.tile(m_f, (1, 2)), np.tile(m_b, (1, 2))]).astype(np.float32), dtype=BF16)
    mask = jnp.asarray(np.stack([np.stack(masks), np.stack([m[::-1, ::-1] for m in masks])]).astype(np.float32))
    return msum, mask, nlev


def _slabs(ref, rows):
    parts = [ref[s, rows, :] for s in range(ref.shape[0])]
    return parts[0] if len(parts) == 1 else jnp.concatenate(parts, axis=-1)


def _split2(g):
    hi = g.astype(BF16)
    lo = (g - hi.astype(F32)).astype(BF16)
    return hi, lo


def _scan_block(backward, gate_input, load_chunk, emit, msum_ref, mask_ref, st_ref, *, nch, c, nlev, state_first):
    msum = msum_ref[0]
    masks = [mask_ref[0, i] > 0.5 for i in range(nlev + 1)]
    starts = [((nch - 1 - ci) if backward else ci) * c for ci in range(nch)]
    order = [slice(r0, r0 + c) for r0 in starts]

    def stage1(rows, z):
        q, k, v, g = load_chunk(rows, z)
        hi, lo = _split2(g * LOG2E)
        return q, k, v.astype(BF16), _dot(msum, jnp.concatenate([hi, lo], axis=0))

    def stage2(q, k, vb, x):
        e = jnp.exp2(x)
        tot = x[(1 + nlev) * c:(1 + nlev) * c + 1]
        dtot = e[(1 + nlev) * c:(1 + nlev) * c + 1]
        qe = (q * e[0:c]).astype(BF16)
        ke = (k * jnp.exp2(tot - x[0:c])).astype(BF16)
        att = jnp.where(masks[0], _dot_nt(q.astype(BF16), k.astype(BF16)), 0.0)
        for lev in range(nlev):
            es = e[(1 + lev) * c:(2 + lev) * c]
            att = jnp.where(masks[1 + lev], _dot_nt((q * es).astype(BF16), (k * es).astype(BF16)), att)
        return qe, dtot, att.astype(BF16), vb, _dot_tn(vb, ke)

    def stage3(rows, st, qe, dtot, att, vb, inc):
        emit(rows, _dot(att, vb) + _dot_nt(qe, st.astype(BF16)))
        return st * dtot + inc

    zs, s1, s2 = {}, {}, {}
    st = st_ref[...]
    for step in range(nch + 3):
        if step < nch:
            zs[step] = gate_input(order[step])
        if 0 <= step - 1 < nch:
            s1[step - 1] = stage1(order[step - 1], zs.pop(step - 1))
        if state_first and 0 <= step - 3 < nch:
            st = stage3(order[step - 3], st, *s2.pop(step - 3))
        if 0 <= step - 2 < nch:
            s2[step - 2] = stage2(*s1.pop(step - 2))
        if not state_first and 0 <= step - 3 < nch:
            st = stage3(order[step - 3], st, *s2.pop(step - 3))
    st_ref[...] = st


def _scan_both_directions(ph, blk, tb, gate_input, load_chunk, ofwd_ref, o_ref, on_ref, gate_ref, **kw):
    c = kw["c"]

    def seq_rows(rows):
        return pl.ds(pl.multiple_of(blk * tb + rows.start, c), c)

    def emit_fwd(rows, o):
        ofwd_ref[seq_rows(rows), :] = o

    def emit_bwd(rows, o):
        o_sum = ofwd_ref[seq_rows(rows), :] + o
        ms = jnp.mean(o_sum * o_sum, axis=-1, keepdims=True)
        o_ref[0, rows, :] = (o_sum * lax.rsqrt(ms + EPS) * on_ref[...] * _silu(_slabs(gate_ref, rows))).astype(BF16)

    pl.when(ph == 0)(lambda: _scan_block(False, gate_input, load_chunk, emit_fwd, **kw))
    pl.when(ph == 1)(lambda: _scan_block(True, gate_input, load_chunk, emit_bwd, **kw))


def _hgrn_kernel(q_ref, z_ref, v_ref, gate_ref, lb_ref, on_ref, msum_ref, mask_ref, o_ref, ofwd_ref, st_ref,
                 *, nblk, tb, c, nlev):
    ph = pl.program_id(2)
    jb = pl.program_id(3)
    blk = jnp.where(ph == 0, jb, nblk - 1 - jb)

    @pl.when(jb == 0)
    def _():
        st_ref[...] = jnp.zeros_like(st_ref)

    log_lb = lb_ref[0, 0, 0:1, :]
    log_1mlb = lb_ref[0, 0, 1:2, :]
    one_mlb = lb_ref[0, 0, 2:3, :]

    def gate_input(rows):
        return _slabs(z_ref, rows)

    def load_chunk(rows, z):
        q = _silu(_slabs(q_ref, rows))
        t = jnp.exp(-jnp.abs(z))
        d = 1.0 + t
        a = log_lb
        b = log_1mlb + (jnp.minimum(z, 0.0) - jnp.log(d))
        g = jnp.maximum(a, b) + jnp.log(1.0 + jnp.exp(-jnp.abs(a - b)))
        k = one_mlb * (jnp.where(z > 0.0, t, 1.0) / d)
        return q, k, _slabs(v_ref, rows), g

    _scan_both_directions(ph, blk, tb, gate_input, load_chunk, ofwd_ref, o_ref, on_ref, gate_ref,
                          msum_ref=msum_ref, mask_ref=mask_ref, st_ref=st_ref, nch=tb // c, c=c, nlev=nlev,
                          state_first=False)


def _hgrn_scan(u, lbp, onorm, b, t):
    tb, c = min(SCAN_BLOCK, t), CHUNK
    nblk = t // tb
    msum, mask, nlev = _scan_consts(c)
    nh = HG_HEADS

    def rows_of(bi, ph, jb):
        return bi * nblk + jnp.where(ph == 0, jb, nblk - 1 - jb)

    def gate_rows(bi, ph, jb):
        return bi * nblk + jnp.where(ph == 0, nblk - 1, nblk - 1 - jb)

    return pl.pallas_call(
        functools.partial(_hgrn_kernel, nblk=nblk, tb=tb, c=c, nlev=nlev),
        grid=(b, nh, 2, nblk),
        in_specs=[
            pl.BlockSpec((1, tb, HG_D), lambda bi, h, ph, jb: (h, rows_of(bi, ph, jb), 0)),
            pl.BlockSpec((1, tb, HG_D), lambda bi, h, ph, jb: (nh + nh * ph + h, rows_of(bi, ph, jb), 0)),
            pl.BlockSpec((1, tb, HG_D), lambda bi, h, ph, jb: (3 * nh + h, rows_of(bi, ph, jb), 0)),
            pl.BlockSpec((1, tb, HG_D), lambda bi, h, ph, jb: (4 * nh + h, gate_rows(bi, ph, jb), 0)),
            pl.BlockSpec((1, 1, 8, HG_D), lambda bi, h, ph, jb: (ph, h, 0, 0)),
            pl.BlockSpec((1, HG_D), lambda bi, h, ph, jb: (0, 0)),
            pl.BlockSpec((1,) + msum.shape[1:], lambda bi, h, ph, jb: (ph, 0, 0)),
            pl.BlockSpec((1,) + mask.shape[1:], lambda bi, h, ph, jb: (ph, 0, 0, 0)),
        ],
        out_specs=pl.BlockSpec(
            (1, tb, HG_D), lambda bi, h, ph, jb: (bi, jnp.where(ph == 0, nblk - 1, nblk - 1 - jb), h)),
        out_shape=jax.ShapeDtypeStruct((b, t, HG_WIDTH), BF16),
        scratch_shapes=[pltpu.VMEM((t, HG_D), F32), pltpu.VMEM((HG_D, HG_D), F32)],
        compiler_params=_cparams(("arbitrary",) * 4),
        name="hgrn_scan",
    )(u, u, u, u, lbp, onorm, msum, mask)


def _gla_kernel(q_ref, k_ref, v_ref, gate_ref, r_ref, w2_ref, gb_ref, on_ref, msum_ref, mask_ref, o_ref,
                ofwd_ref, st_ref, *, nblk, tb, c, nlev):
    ph = pl.program_id(2)
    jb = pl.program_id(3)
    blk = jnp.where(ph == 0, jb, nblk - 1 - jb)

    @pl.when(jb == 0)
    def _():
        st_ref[...] = jnp.zeros_like(st_ref)

    w2 = w2_ref[0, 0]
    gb = gb_ref[0, 0]

    def gate_input(rows):
        return _dot(r_ref[0, rows, :].astype(BF16), w2) + gb

    def load_chunk(rows, z):
        q = _slabs(q_ref, rows) * (GLA_DK ** -0.5)
        g = _log_sigmoid(z) * (1.0 / GLA_GATE_NORM)
        return q, _slabs(k_ref, rows), _slabs(v_ref, rows), g

    _scan_both_directions(ph, blk, tb, gate_input, load_chunk, ofwd_ref, o_ref, on_ref, gate_ref,
                          msum_ref=msum_ref, mask_ref=mask_ref, st_ref=st_ref, nch=tb // c, c=c, nlev=nlev,
                          state_first=True)


def _gla_scan(u, r, w2p, gbias, onorm, b, t):
    tb, c = min(SCAN_BLOCK, t), CHUNK
    nblk = t // tb
    msum, mask, nlev = _scan_consts(c)
    nh = GLA_HEADS
    ks, vs = GLA_DK // LANE, GLA_DV // LANE
    vblk0 = 2 * nh * ks // vs

    def blk_of(ph, jb):
        return jnp.where(ph == 0, jb, nblk - 1 - jb)

    def rows_of(bi, ph, jb):
        return bi * nblk + blk_of(ph, jb)

    def gate_rows(bi, ph, jb):
        return bi * nblk + jnp.where(ph == 0, nblk - 1, nblk - 1 - jb)

    return pl.pallas_call(
        functools.partial(_gla_kernel, nblk=nblk, tb=tb, c=c, nlev=nlev),
        grid=(b, nh, 2, nblk),
        in_specs=[
            pl.BlockSpec((ks, tb, LANE), lambda bi, h, ph, jb: (h, rows_of(bi, ph, jb), 0)),
            pl.BlockSpec((ks, tb, LANE), lambda bi, h, ph, jb: (nh + h, rows_of(bi, ph, jb), 0)),
            pl.BlockSpec((vs, tb, LANE), lambda bi, h, ph, jb: (vblk0 + h, rows_of(bi, ph, jb), 0)),
            pl.BlockSpec((vs, tb, LANE), lambda bi, h, ph, jb: (vblk0 + nh + h, gate_rows(bi, ph, jb), 0)),
            pl.BlockSpec((1, tb, GLA_R_PAD), lambda bi, h, ph, jb: (bi, blk_of(ph, jb), 0)),
            pl.BlockSpec((1, 1, GLA_R_PAD, GLA_DK), lambda bi, h, ph, jb: (ph, h, 0, 0)),
            pl.BlockSpec((1, 1, 1, GLA_DK), lambda bi, h, ph, jb: (ph, h, 0, 0)),
            pl.BlockSpec((1, GLA_DV), lambda bi, h, ph, jb: (0, 0)),
            pl.BlockSpec((1,) + msum.shape[1:], lambda bi, h, ph, jb: (ph, 0, 0)),
            pl.BlockSpec((1,) + mask.shape[1:], lambda bi, h, ph, jb: (ph, 0, 0, 0)),
        ],
        out_specs=pl.BlockSpec(
            (1, tb, GLA_DV), lambda bi, h, ph, jb: (bi, jnp.where(ph == 0, nblk - 1, nblk - 1 - jb), h)),
        out_shape=jax.ShapeDtypeStruct((b, t, nh * GLA_DV), BF16),
        scratch_shapes=[pltpu.VMEM((t, GLA_DV), F32), pltpu.VMEM((GLA_DV, GLA_DK), F32)],
        compiler_params=_cparams(("arbitrary",) * 4),
        name="gla_scan",
    )(u, u, u, u, r, w2p, gbias, onorm, msum, mask)


def _rope128(x, cos, sin):
    lane = lax.broadcasted_iota(jnp.int32, x.shape, 1)
    half = MLA_ROPE // 2
    swapped = jnp.where(lane < half, pltpu.roll(x, LANE - half, 1), pltpu.roll(x, half, 1))
    return x * cos + swapped * sin


def _mla_prep_kernel(c_ref, cos_ref, sin_ref, qag_ref, kvag_ref, wq_ref, wk_ref, wv_ref, gq_ref, gk_ref,
                     q_ref, k_ref, v_ref):
    c = c_ref[...]
    cq = c[:, :Q_LORA]
    ckv = c[:, Q_LORA:Q_LORA + KV_LORA]
    kpe = c[:, Q_LORA + KV_LORA:]
    cqn = (cq * lax.rsqrt(jnp.mean(cq * cq, axis=-1, keepdims=True) + EPS) * qag_ref[...]).astype(BF16)
    ckvn = (ckv * lax.rsqrt(jnp.mean(ckv * ckv, axis=-1, keepdims=True) + EPS) * kvag_ref[...]).astype(BF16)
    cos = cos_ref[...]
    sin = sin_ref[...]
    gq = gq_ref[...]
    gk = gk_ref[...]
    q_raw = _dot(cqn, wq_ref[...])
    kn_raw = _dot(ckvn, wk_ref[...])
    v_ref[...] = _dot(ckvn, wv_ref[...]).astype(BF16)
    kpe_ss = jnp.sum(kpe * kpe, axis=-1, keepdims=True)
    kpe_rot = _rope128(kpe * gk[:, LANE:], cos, sin)
    for h in range(MLA_HEADS):
        lo = h * MLA_QK_PAD
        qn = q_raw[:, lo:lo + LANE]
        qr = q_raw[:, lo + LANE:lo + 2 * LANE]
        ss = jnp.sum(qn * qn, axis=-1, keepdims=True) + jnp.sum(qr * qr, axis=-1, keepdims=True)
        rinv = lax.rsqrt(ss * (1.0 / MLA_QK) + EPS) * ATTN_Q_SCALE
        q_ref[:, lo:lo + LANE] = (qn * rinv * gq[:, :LANE]).astype(BF16)
        q_ref[:, lo + LANE:lo + 2 * LANE] = _rope128(qr * rinv * gq[:, LANE:], cos, sin).astype(BF16)
        kn = kn_raw[:, h * LANE:(h + 1) * LANE]
        ssk = jnp.sum(kn * kn, axis=-1, keepdims=True) + kpe_ss
        rinvk = lax.rsqrt(ssk * (1.0 / MLA_QK) + EPS)
        k_ref[:, lo:lo + LANE] = (kn * rinvk * gk[:, :LANE]).astype(BF16)
        k_ref[:, lo + LANE:lo + 2 * LANE] = (kpe_rot * rinvk).astype(BF16)


def _mla_prep(c, cos, sin, qag, kvag, wq, wk, wv, gq, gk, t):
    n = c.shape[0]
    tm = min(256, t)
    tps = t // tm
    full = lambda a: pl.BlockSpec(a.shape, lambda i: (0,) * a.ndim)
    return pl.pallas_call(
        _mla_prep_kernel,
        grid=(n // tm,),
        in_specs=[
            pl.BlockSpec((tm, MLA_C_PAD), lambda i: (i, 0)),
            pl.BlockSpec((tm, LANE), lambda i: (i % tps, 0)),
            pl.BlockSpec((tm, LANE), lambda i: (i % tps, 0)),
            full(qag), full(kvag), full(wq), full(wk), full(wv), full(gq), full(gk),
        ],
        out_specs=[
            pl.BlockSpec((tm, MLA_HEADS * MLA_QK_PAD), lambda i: (i, 0)),
            pl.BlockSpec((tm, MLA_HEADS * MLA_QK_PAD), lambda i: (i, 0)),
            pl.BlockSpec((tm, MLA_HEADS * MLA_V), lambda i: (i, 0)),
        ],
        out_shape=[
            jax.ShapeDtypeStruct((n, MLA_HEADS * MLA_QK_PAD), BF16),
            jax.ShapeDtypeStruct((n, MLA_HEADS * MLA_QK_PAD), BF16),
            jax.ShapeDtypeStruct((n, MLA_HEADS * MLA_V), BF16),
        ],
        compiler_params=_cparams(("arbitrary",)),
        name="mla_prep",
    )(c, cos, sin, qag, kvag, wq, wk, wv, gq, gk)


def _attn_kernel(q_ref, k_ref, v_ref, o_ref, *, tk, nk, nsplit):
    tq = q_ref.shape[1]
    th = tq // nsplit
    qs = [q_ref[0, i * th:(i + 1) * th, :] for i in range(nsplit)]

    def scores(ci):
        kc = k_ref[0, ci * tk:(ci + 1) * tk, :]
        return [_dot_nt(q, kc) for q in qs]

    def update(ci, s_list, state):
        vc = v_ref[0, ci * tk:(ci + 1) * tk, :]
        out = []
        for s, (m, l, acc) in zip(s_list, state):
            m_new = jnp.maximum(m, jnp.max(s, axis=-1, keepdims=True))
            alpha = jnp.exp2(m - m_new)
            p = jnp.exp2(s - m_new)
            l = alpha * l + jnp.sum(p, axis=-1, keepdims=True)
            acc = alpha * acc + _dot(p.astype(BF16), vc)
            out.append((m_new, l, acc))
        return out

    state = [(jnp.full((th, 1), -jnp.inf, F32), jnp.zeros((th, 1), F32), jnp.zeros((th, MLA_V), F32))
             for _ in range(nsplit)]
    s_cur = scores(0)
    for ci in range(nk):
        s_next = scores(ci + 1) if ci + 1 < nk else None
        state = update(ci, s_cur, state)
        s_cur = s_next
    for i, (_, l, acc) in enumerate(state):
        o_ref[0, i * th:(i + 1) * th, :] = (acc / l).astype(BF16)


def _attention(q, k, v, b, t):
    tq, tk = min(ATTN_TQ, t), min(ATTN_TK, t)
    return pl.pallas_call(
        functools.partial(_attn_kernel, tk=tk, nk=t // tk, nsplit=ATTN_SPLIT),
        grid=(b, MLA_HEADS, t // tq),
        in_specs=[
            pl.BlockSpec((1, tq, MLA_QK_PAD), lambda bi, h, i: (bi, i, h)),
            pl.BlockSpec((1, t, MLA_QK_PAD), lambda bi, h, i: (bi, 0, h)),
            pl.BlockSpec((1, t, MLA_V), lambda bi, h, i: (bi, 0, h)),
        ],
        out_specs=pl.BlockSpec((1, tq, MLA_V), lambda bi, h, i: (bi, i, h)),
        out_shape=jax.ShapeDtypeStruct((b, t, MLA_HEADS * MLA_V), BF16),
        compiler_params=_cparams(("arbitrary",) * 3),
        name="mla_attention",
    )(q, k, v)


def _rope_tables(t):
    half = MLA_ROPE // 2
    inv_freq = ROPE_THETA ** (-jnp.arange(half, dtype=F32) / half)
    ang = jnp.arange(t, dtype=jnp.int32).astype(F32)[:, None] * inv_freq[None, :]
    cos, sin = jnp.cos(ang), jnp.sin(ang)
    zeros = jnp.zeros((t, LANE - MLA_ROPE), F32)
    return (jnp.concatenate([cos, cos, zeros], axis=-1), jnp.concatenate([-sin, sin, zeros], axis=-1))


def _pad_cols(a, width):
    return jnp.pad(a, [(0, 0)] * (a.ndim - 1) + [(0, width - a.shape[-1])])


def _tile_cols(w, tn):
    k, n = w.shape[-2:]
    lead = w.shape[:-2]
    w = w.reshape(lead + (k, n // tn, tn))
    return jnp.swapaxes(w, -3, -2)


def _prep_even(e, w):
    ev_in = w['ev_w_in'][e]
    w_hg = _tile_cols(ev_in[:, :5 * HG_WIDTH].astype(BF16), INPROJ_TILE)
    w_mla = _pad_cols(ev_in[:, 5 * HG_WIDTH:], MLA_C_PAD).astype(BF16)
    uq = w['mla_w_uq'][e].reshape(Q_LORA, MLA_HEADS, MLA_QK)
    wq = _pad_cols(uq, MLA_QK_PAD).reshape(Q_LORA, MLA_HEADS * MLA_QK_PAD).astype(BF16)
    ukv = w['mla_w_ukv'][e].reshape(KV_LORA, MLA_HEADS, MLA_NOPE + MLA_V)
    wk = ukv[:, :, :MLA_NOPE].reshape(KV_LORA, MLA_HEADS * MLA_NOPE).astype(BF16)
    wv = ukv[:, :, MLA_NOPE:].reshape(KV_LORA, MLA_HEADS * MLA_V).astype(BF16)
    p = jax.nn.softmax(w['hgrn_lb'].astype(F32), axis=1)
    lb = jnp.cumsum(p, axis=1)
    lb = (lb - lb[:, :1])[:, e].reshape(2, HG_HEADS, 1, HG_D)
    lbp = jnp.concatenate([jnp.log(lb), jnp.log1p(-lb), 1.0 - lb, jnp.zeros((2, HG_HEADS, 5, HG_D), F32)], axis=2)
    w_out = w['ev_w_out'][e].astype(BF16)
    return dict(
        w_hg=w_hg, w_mla=w_mla, wq=wq, wk=wk, wv=wv, lbp=lbp,
        hg_onorm=w['hgrn_onorm_g'][e].reshape(1, HG_D),
        qag=w['mla_qa_norm_g'][e].reshape(1, Q_LORA), kvag=w['mla_kva_norm_g'][e].reshape(1, KV_LORA),
        gq=_pad_cols(w['mla_qn_g'][e].reshape(1, MLA_QK), MLA_QK_PAD),
        gk=_pad_cols(w['mla_kn_g'][e].reshape(1, MLA_QK), MLA_QK_PAD),
        w_out_hg=w_out[:HG_WIDTH], w_out_mla=w_out[HG_WIDTH:],
    )


def _prep_odd(e, w):
    od_in = w['od_w_in'][e]
    w_main = _tile_cols(od_in[:, :GLA_MAIN].astype(BF16), INPROJ_TILE)
    w_r = _pad_cols(od_in[:, GLA_MAIN:], GLA_R_PAD).astype(BF16)
    w2 = w['gla_gk_w2'][e].reshape(2, GLA_RANK, GLA_HEADS, GLA_DK).transpose(0, 2, 1, 3)
    w2p = jnp.zeros((2, GLA_HEADS, GLA_R_PAD, GLA_DK), F32)
    w2p = w2p.at[0, :, :GLA_RANK].set(w2[0]).at[1, :, GLA_RANK:2 * GLA_RANK].set(w2[1]).astype(BF16)
    gbias = w['gla_gk_b'][e].reshape(2, GLA_HEADS, 1, GLA_DK)
    return dict(w_main=w_main, w_r=w_r, w2p=w2p, gbias=gbias,
                onorm=w['gla_onorm_g'][e].reshape(1, GLA_DV), w_out=w['od_w_out'][e].astype(BF16))


def _trunk(x3, boff, nb_total, table, norm_g, w13, w2, evens, odds):
    b, t, _ = x3.shape
    n = b * t
    x = x3.reshape(n, D_MODEL)
    mod = _Mod(table, norm_g, nb_total, boff, t)
    cos, sin = _rope_tables(t)
    for layer in range(DEPTH):
        x = _ffn(x, mod, layer, 0, 0, w13, w2)
        e = layer // 2
        if layer % 2 == 0:
            p = evens[e]
            u_hg, c_mla = _inproj(x, mod, layer, p['w_hg'], p['w_mla'])
            o_hg = _hgrn_scan(u_hg, p['lbp'], p['hg_onorm'], b, t)
            q, k, v = _mla_prep(c_mla, cos, sin, p['qag'], p['kvag'], p['wq'], p['wk'], p['wv'], p['gq'], p['gk'], t)
            o_mla = _attention(q.reshape(b, t, -1), k.reshape(b, t, -1), v.reshape(b, t, -1), b, t)
            x = _outproj(x, mod, layer, [o_hg.reshape(n, HG_WIDTH), o_mla.reshape(n, MLA_HEADS * MLA_V)],
                         [p['w_out_hg'], p['w_out_mla']])
        else:
            p = odds[e]
            u, r = _inproj(x, mod, layer, p['w_main'], p['w_r'])
            o = _gla_scan(u, r.reshape(b, t, GLA_R_PAD), p['w2p'], p['gbias'], p['onorm'], b, t)
            x = _outproj(x, mod, layer, [o.reshape(n, GLA_HEADS * GLA_DV)], [p['w_out']])
        x = _ffn(x, mod, layer, 2, 1, w13, w2)
    return x.reshape(b, t, D_MODEL)


def kernel(x_prompt, x_sample, c_prompt, c_sample, ada_w, ada_b, norm_g, ffn_w13, ffn_w2, ev_w_in, ev_w_out, hgrn_lb, hgrn_onorm_g, mla_qa_norm_g, mla_w_uq, mla_kva_norm_g, mla_w_ukv, mla_qn_g, mla_kn_g, od_w_in, od_w_out, gla_gk_w2, gla_gk_b, gla_onorm_g):
    w = dict(ev_w_in=ev_w_in, ev_w_out=ev_w_out, hgrn_lb=hgrn_lb, hgrn_onorm_g=hgrn_onorm_g,
             mla_qa_norm_g=mla_qa_norm_g, mla_w_uq=mla_w_uq, mla_kva_norm_g=mla_kva_norm_g,
             mla_w_ukv=mla_w_ukv, mla_qn_g=mla_qn_g, mla_kn_g=mla_kn_g, od_w_in=od_w_in,
             od_w_out=od_w_out, gla_gk_w2=gla_gk_w2, gla_gk_b=gla_gk_b, gla_onorm_g=gla_onorm_g)
    bp, bs = x_prompt.shape[0], x_sample.shape[0]
    nb_total = bp + bs
    c_all = jnp.concatenate([c_prompt, c_sample], axis=0)
    mod = _ada_mod(c_all, ada_w, ada_b)
    table = mod.reshape(DEPTH * nb_total * 3 * N_SUB, 1, D_MODEL)
    ng = norm_g.reshape(DEPTH * N_SUB, 1, D_MODEL)
    w13 = ffn_w13.astype(BF16)
    w2 = ffn_w2.astype(BF16)
    evens = [_prep_even(e, w) for e in range((DEPTH + 1) // 2)]
    odds = [_prep_odd(e, w) for e in range(DEPTH // 2)]
    y_prompt = _trunk(x_prompt, 0, nb_total, table, ng, w13, w2, evens, odds)
    y_sample = _trunk(x_sample, bp, nb_total, table, ng, w13, w2, evens, odds)
    return (y_prompt, y_sample)
```

```python
import functools
import math

import numpy as np
import jax
import jax.numpy as jnp
from jax import lax
from jax.experimental import pallas as pl
from jax.experimental.pallas import tpu as pltpu

F32 = jnp.float32
BF16 = jnp.bfloat16

D_MODEL = 2048
DEPTH = 4
N_SUB = 3
EPS = 1e-6
FFN_DIM = 5632

HG_HEADS = 8
HG_D = 128
HG_WIDTH = HG_HEADS * HG_D

MLA_HEADS = 8
MLA_NOPE = 128
MLA_ROPE = 64
MLA_QK = MLA_NOPE + MLA_ROPE
MLA_V = 128
MLA_QK_PAD = 256
Q_LORA = 512
KV_LORA = 512
MLA_C_PAD = 1152
ROPE_THETA = 10000.0

GLA_HEADS = 4
GLA_DK = 256
GLA_DV = 512
GLA_RANK = 16
GLA_GATE_NORM = 16.0
GLA_MAIN = 2 * GLA_HEADS * GLA_DK + 2 * GLA_HEADS * GLA_DV
GLA_R_PAD = 128

CHUNK = 64
LANE = 128
VMEM_LIMIT = 56 * 1024 * 1024

TOKEN_TILE = 512
FFN_TOKEN_TILE = 1024
FFN_TILE = 512
INPROJ_TOKEN_TILE = 1024
INPROJ_TILE = 512
ROW_GROUPS = 2
SCAN_BLOCK = 1024
ATTN_TQ = 512
ATTN_SPLIT = 2
ATTN_TK = 1024
LOG2E = math.log2(math.e)
ATTN_Q_SCALE = MLA_QK ** -0.5 * LOG2E


def _cparams(sem):
    return pltpu.CompilerParams(dimension_semantics=sem, vmem_limit_bytes=VMEM_LIMIT)


def _silu(x):
    return x * jax.nn.sigmoid(x)


def _log_sigmoid(z):
    return jnp.minimum(z, 0.0) - jnp.log(1.0 + jnp.exp(-jnp.abs(z)))


def _adaln(x, g, scale, shift):
    ms = jnp.mean(x * x, axis=-1, keepdims=True)
    y = x * lax.rsqrt(ms + EPS) * g
    return y * (1.0 + scale) + shift


def _dot(a, b):
    return jnp.dot(a, b, preferred_element_type=F32)


def _dot_nt(a, b):
    return lax.dot_general(a, b, (((1,), (1,)), ((), ())), preferred_element_type=F32)


def _dot_tn(a, b):
    return lax.dot_general(a, b, (((0,), (0,)), ((), ())), preferred_element_type=F32)


def _ada_kernel(c_ref, w_ref, b_ref, o_ref):
    cond = _silu(c_ref[...]).astype(BF16)
    o_ref[0] = _dot(cond, w_ref[0].astype(BF16)) + b_ref[0]


def _ada_mod(c_all, ada_w, ada_b):
    nb = c_all.shape[0]
    n_out = ada_w.shape[-1]
    tn = 1024
    return pl.pallas_call(
        _ada_kernel,
        grid=(DEPTH, n_out // tn),
        in_specs=[
            pl.BlockSpec((nb, D_MODEL), lambda l, j: (0, 0)),
            pl.BlockSpec((1, D_MODEL, tn), lambda l, j: (l, 0, j)),
            pl.BlockSpec((1, 1, tn), lambda l, j: (l, 0, j)),
        ],
        out_specs=pl.BlockSpec((1, nb, tn), lambda l, j: (l, 0, j)),
        out_shape=jax.ShapeDtypeStruct((DEPTH, nb, n_out), F32),
        compiler_params=_cparams(("arbitrary", "arbitrary")),
        name="ada_mod",
    )(c_all, ada_w, ada_b.reshape(DEPTH, 1, n_out))


class _Mod:
    def __init__(self, table, norm_g, nb_total, boff, seq_len):
        self.table = table
        self.norm_g = norm_g
        self.nb_total = nb_total
        self.boff = boff
        self.seq_len = seq_len

    def spec(self, layer, sub, kind, tile):
        nb, boff, tps = self.nb_total, self.boff, self.seq_len // tile
        return pl.BlockSpec(
            (1, 1, D_MODEL),
            lambda i, *_: (((layer * nb + boff + i // tps) * 9 + sub * 3 + kind), 0, 0))

    def norm_spec(self, layer, sub):
        return pl.BlockSpec((1, 1, D_MODEL), lambda i, *_: (layer * N_SUB + sub, 0, 0))


def _ffn_kernel(x_ref, sh_ref, sc_ref, gt_ref, ng_ref, w1_ref, w3_ref, w2_ref, o_ref, h_ref, *, nj, ngroups):
    acc_ref = o_ref
    j = pl.program_id(1)
    rg = x_ref.shape[0] // ngroups

    def body(first, last):
        w1, w3, w2 = w1_ref[...], w3_ref[...], w2_ref[...]
        ups = []
        for g in range(ngroups):
            rows = slice(g * rg, (g + 1) * rg)
            if first:
                h = _adaln(x_ref[rows, :], ng_ref[0], sc_ref[0], sh_ref[0]).astype(BF16)
                h_ref[rows, :] = h
            else:
                h = h_ref[rows, :]
            ups.append((_dot(h, w1), _dot(h, w3)))
        for g, (a, u) in enumerate(ups):
            rows = slice(g * rg, (g + 1) * rg)
            down = _dot((_silu(a) * u).astype(BF16), w2)
            acc = down if first else acc_ref[rows, :] + down
            if last:
                o_ref[rows, :] = x_ref[rows, :] + (0.5 * gt_ref[0]) * acc
            else:
                acc_ref[rows, :] = acc

    pl.when(j == 0)(lambda: body(True, False))
    pl.when(jnp.logical_and(j > 0, j < nj - 1))(lambda: body(False, False))
    pl.when(j == nj - 1)(lambda: body(False, True))


def _ffn(x, mod, layer, sub, which, w13, w2):
    n = x.shape[0]
    tm, tf = min(FFN_TOKEN_TILE, mod.seq_len), FFN_TILE
    nj = FFN_DIM // tf
    return pl.pallas_call(
        functools.partial(_ffn_kernel, nj=nj, ngroups=ROW_GROUPS),
        grid=(n // tm, nj),
        in_specs=[
            pl.BlockSpec((tm, D_MODEL), lambda i, j: (i, 0)),
            mod.spec(layer, sub, 0, tm), mod.spec(layer, sub, 1, tm), mod.spec(layer, sub, 2, tm),
            mod.norm_spec(layer, sub),
            pl.BlockSpec((None, None, D_MODEL, tf), lambda i, j: (layer, which, 0, j)),
            pl.BlockSpec((None, None, D_MODEL, tf), lambda i, j: (layer, which, 0, j + nj)),
            pl.BlockSpec((None, None, tf, D_MODEL), lambda i, j: (layer, which, j, 0)),
        ],
        out_specs=pl.BlockSpec((tm, D_MODEL), lambda i, j: (i, 0)),
        out_shape=jax.ShapeDtypeStruct((n, D_MODEL), F32),
        scratch_shapes=[pltpu.VMEM((tm, D_MODEL), BF16)],
        compiler_params=_cparams(("arbitrary", "arbitrary")),
        name="ffn",
    )(x, mod.table, mod.table, mod.table, mod.norm_g, w13, w13, w2)


def _inproj_kernel(x_ref, sh_ref, sc_ref, ng_ref, wm_ref, ws_ref, om_ref, os_ref, h_ref, *, ngroups):
    j = pl.program_id(1)
    rg = x_ref.shape[0] // ngroups

    def store_main(rows, res):
        for s in range(om_ref.shape[0]):
            om_ref[s, rows, :] = res[:, s * LANE:(s + 1) * LANE]

    @pl.when(j == 0)
    def _():
        for g in range(ngroups):
            rows = slice(g * rg, (g + 1) * rg)
            h = _adaln(x_ref[rows, :], ng_ref[0], sc_ref[0], sh_ref[0]).astype(BF16)
            h_ref[rows, :] = h
            store_main(rows, _dot(h, wm_ref[...]))
            os_ref[rows, :] = _dot(h, ws_ref[...])

    @pl.when(j > 0)
    def _():
        store_main(slice(None), _dot(h_ref[...], wm_ref[...]))


def _inproj(x, mod, layer, w_main, w_small):
    n = x.shape[0]
    tm = min(INPROJ_TOKEN_TILE, mod.seq_len)
    ntiles, _, tn = w_main.shape
    n_main, n_small = ntiles * tn, w_small.shape[1]
    return pl.pallas_call(
        functools.partial(_inproj_kernel, ngroups=ROW_GROUPS),
        grid=(n // tm, ntiles),
        in_specs=[
            pl.BlockSpec((tm, D_MODEL), lambda i, j: (i, 0)),
            mod.spec(layer, 1, 0, tm), mod.spec(layer, 1, 1, tm),
            mod.norm_spec(layer, 1),
            pl.BlockSpec((None, D_MODEL, tn), lambda i, j: (j, 0, 0)),
            pl.BlockSpec((D_MODEL, n_small), lambda i, j: (0, 0)),
        ],
        out_specs=[
            pl.BlockSpec((tn // LANE, tm, LANE), lambda i, j: (j, i, 0)),
            pl.BlockSpec((tm, n_small), lambda i, j: (i, 0)),
        ],
        out_shape=[jax.ShapeDtypeStruct((n_main // LANE, n, LANE), F32), jax.ShapeDtypeStruct((n, n_small), F32)],
        scratch_shapes=[pltpu.VMEM((tm, D_MODEL), BF16)],
        compiler_params=_cparams(("arbitrary", "arbitrary")),
        name="inproj",
    )(x, mod.table, mod.table, mod.norm_g, w_main, w_small)


def _outproj_kernel(x_ref, gt_ref, *refs):
    o_ref = refs[-1]
    npair = (len(refs) - 1) // 2
    y = _dot(refs[0][...], refs[npair][...])
    for p in range(1, npair):
        y = y + _dot(refs[p][...], refs[npair + p][...])
    o_ref[...] = x_ref[...] + gt_ref[0] * y


def _outproj(x, mod, layer, mixes, ws):
    n = x.shape[0]
    tm = TOKEN_TILE
    in_specs = [pl.BlockSpec((tm, D_MODEL), lambda i: (i, 0)), mod.spec(layer, 1, 2, tm)]
    in_specs += [pl.BlockSpec((tm, m.shape[1]), lambda i: (i, 0)) for m in mixes]
    in_specs += [pl.BlockSpec(w.shape, lambda i: (0, 0)) for w in ws]
    return pl.pallas_call(
        _outproj_kernel,
        grid=(n // tm,),
        in_specs=in_specs,
        out_specs=pl.BlockSpec((tm, D_MODEL), lambda i: (i, 0)),
        out_shape=jax.ShapeDtypeStruct((n, D_MODEL), F32),
        compiler_params=_cparams(("arbitrary",)),
        name="outproj",
    )(x, mod.table, *mixes, *ws)


def _scan_consts(c):
    nlev = int(math.log2(c))
    t = np.arange(c)
    row, col = t[:, None], t[None, :]
    blocks = [col <= row]
    masks = [np.eye(c, dtype=bool)]
    for lev in range(nlev):
        s = 1 << lev
        blk = t // s
        odd = (blk % 2) == 1
        bstart = (blk * s)[:, None]
        bend = bstart + s - 1
        as_query = (col >= bstart) & (col <= row)
        as_key = (col > row) & (col <= bend)
        blocks.append(np.where(odd[:, None], as_query, as_key))
        masks.append(odd[:, None] & (~odd[None, :]) & ((row // (2 * s)) == (col // (2 * s))))
    ones = np.ones((8, c), dtype=bool)
    m_f = np.concatenate(blocks + [ones], axis=0)
    m_b = np.concatenate([b[::-1, ::-1] for b in blocks] + [ones], axis=0)
    msum = jnp.asarray(np.stack([np.tile(m_f, (1, 2)), np.tile(m_b, (1, 2))]).astype(np.float32), dtype=BF16)
    mask = jnp.asarray(np.stack([np.stack(masks), np.stack([m[::-1, ::-1] for m in masks])]).astype(np.float32))
    return msum, mask, nlev


def _slabs(ref, rows):
    parts = [ref[s, rows, :] for s in range(ref.shape[0])]
    return parts[0] if len(parts) == 1 else jnp.concatenate(parts, axis=-1)


def _split2(g):
    hi = g.astype(BF16)
    lo = (g - hi.astype(F32)).astype(BF16)
    return hi, lo


def _scan_block(backward, gate_input, load_chunk, emit, msum_ref, mask_ref, st_ref, *, nch, c, nlev, state_first):
    msum = msum_ref[0]
    masks = [mask_ref[0, i] > 0.5 for i in range(nlev + 1)]
    starts = [((nch - 1 - ci) if backward else ci) * c for ci in range(nch)]
    order = [slice(r0, r0 + c) for r0 in starts]

    def stage1(rows, z):
        q, k, v, g = load_chunk(rows, z)
        hi, lo = _split2(g * LOG2E)
        return q, k, v.astype(BF16), _dot(msum, jnp.concatenate([hi, lo], axis=0))

    def stage2(q, k, vb, x):
        e = jnp.exp2(x)
        tot = x[(1 + nlev) * c:(1 + nlev) * c + 1]
        dtot = e[(1 + nlev) * c:(1 + nlev) * c + 1]
        qe = (q * e[0:c]).astype(BF16)
        ke = (k * jnp.exp2(tot - x[0:c])).astype(BF16)
        qb, kb = q.astype(BF16), k.astype(BF16)
        eb = e[c:(1 + nlev) * c].astype(BF16)
        att = jnp.where(masks[0], _dot_nt(qb, kb), 0.0)
        for lev in range(nlev):
            es = eb[lev * c:(lev + 1) * c]
            att = jnp.where(masks[1 + lev], _dot_nt(qb * es, kb * es), att)
        return qe, dtot, att.astype(BF16), vb, _dot_tn(vb, ke)

    def stage3(rows, st, qe, dtot, att, vb, inc):
        emit(rows, _dot(att, vb) + _dot_nt(qe, st.astype(BF16)))
        return st * dtot + inc

    zs, s1, s2 = {}, {}, {}
    st = st_ref[...]
    for step in range(nch + 3):
        if step < nch:
            zs[step] = gate_input(order[step])
        if 0 <= step - 1 < nch:
            s1[step - 1] = stage1(order[step - 1], zs.pop(step - 1))
        if state_first and 0 <= step - 3 < nch:
            st = stage3(order[step - 3], st, *s2.pop(step - 3))
        if 0 <= step - 2 < nch:
            s2[step - 2] = stage2(*s1.pop(step - 2))
        if not state_first and 0 <= step - 3 < nch:
            st = stage3(order[step - 3], st, *s2.pop(step - 3))
    st_ref[...] = st


def _scan_both_directions(ph, blk, tb, gate_input, load_chunk, ofwd_ref, o_ref, on_ref, gate_ref, **kw):
    c = kw["c"]

    def seq_rows(rows):
        return pl.ds(pl.multiple_of(blk * tb + rows.start, c), c)

    def emit_fwd(rows, o):
        ofwd_ref[seq_rows(rows), :] = o

    def emit_bwd(rows, o):
        o_sum = ofwd_ref[seq_rows(rows), :] + o
        ms = jnp.mean(o_sum * o_sum, axis=-1, keepdims=True)
        o_ref[0, rows, :] = (o_sum * lax.rsqrt(ms + EPS) * on_ref[...] * _silu(_slabs(gate_ref, rows))).astype(BF16)

    pl.when(ph == 0)(lambda: _scan_block(False, gate_input, load_chunk, emit_fwd, **kw))
    pl.when(ph == 1)(lambda: _scan_block(True, gate_input, load_chunk, emit_bwd, **kw))


def _hgrn_kernel(q_ref, z_ref, v_ref, gate_ref, lb_ref, on_ref, msum_ref, mask_ref, o_ref, ofwd_ref, st_ref,
                 *, nblk, tb, c, nlev):
    ph = pl.program_id(2)
    jb = pl.program_id(3)
    blk = jnp.where(ph == 0, jb, nblk - 1 - jb)

    @pl.when(jb == 0)
    def _():
        st_ref[...] = jnp.zeros_like(st_ref)

    log_lb = lb_ref[0, 0, 0:1, :]
    log_1mlb = lb_ref[0, 0, 1:2, :]
    one_mlb = lb_ref[0, 0, 2:3, :]

    def gate_input(rows):
        return _slabs(z_ref, rows)

    def load_chunk(rows, z):
        q = _silu(_slabs(q_ref, rows))
        t = jnp.exp(-jnp.abs(z))
        d = 1.0 + t
        a = log_lb
        b = log_1mlb + (jnp.minimum(z, 0.0) - jnp.log(d))
        g = jnp.maximum(a, b) + jnp.log(1.0 + jnp.exp(-jnp.abs(a - b)))
        k = one_mlb * (jnp.where(z > 0.0, t, 1.0) / d)
        return q, k, _slabs(v_ref, rows), g

    _scan_both_directions(ph, blk, tb, gate_input, load_chunk, ofwd_ref, o_ref, on_ref, gate_ref,
                          msum_ref=msum_ref, mask_ref=mask_ref, st_ref=st_ref, nch=tb // c, c=c, nlev=nlev,
                          state_first=False)


def _hgrn_scan(u, lbp, onorm, b, t):
    tb, c = min(SCAN_BLOCK, t), CHUNK
    nblk = t // tb
    msum, mask, nlev = _scan_consts(c)
    nh = HG_HEADS

    def rows_of(bi, ph, jb):
        return bi * nblk + jnp.where(ph == 0, jb, nblk - 1 - jb)

    def gate_rows(bi, ph, jb):
        return bi * nblk + jnp.where(ph == 0, nblk - 1, nblk - 1 - jb)

    return pl.pallas_call(
        functools.partial(_hgrn_kernel, nblk=nblk, tb=tb, c=c, nlev=nlev),
        grid=(b, nh, 2, nblk),
        in_specs=[
            pl.BlockSpec((1, tb, HG_D), lambda bi, h, ph, jb: (h, rows_of(bi, ph, jb), 0)),
            pl.BlockSpec((1, tb, HG_D), lambda bi, h, ph, jb: (nh + nh * ph + h, rows_of(bi, ph, jb), 0)),
            pl.BlockSpec((1, tb, HG_D), lambda bi, h, ph, jb: (3 * nh + h, rows_of(bi, ph, jb), 0)),
            pl.BlockSpec((1, tb, HG_D), lambda bi, h, ph, jb: (4 * nh + h, gate_rows(bi, ph, jb), 0)),
            pl.BlockSpec((1, 1, 8, HG_D), lambda bi, h, ph, jb: (ph, h, 0, 0)),
            pl.BlockSpec((1, HG_D), lambda bi, h, ph, jb: (0, 0)),
            pl.BlockSpec((1,) + msum.shape[1:], lambda bi, h, ph, jb: (ph, 0, 0)),
            pl.BlockSpec((1,) + mask.shape[1:], lambda bi, h, ph, jb: (ph, 0, 0, 0)),
        ],
        out_specs=pl.BlockSpec(
            (1, tb, HG_D), lambda bi, h, ph, jb: (bi, jnp.where(ph == 0, nblk - 1, nblk - 1 - jb), h)),
        out_shape=jax.ShapeDtypeStruct((b, t, HG_WIDTH), BF16),
        scratch_shapes=[pltpu.VMEM((t, HG_D), F32), pltpu.VMEM((HG_D, HG_D), F32)],
        compiler_params=_cparams(("arbitrary",) * 4),
        name="hgrn_scan",
    )(u, u, u, u, lbp, onorm, msum, mask)


def _gla_kernel(q_ref, k_ref, v_ref, gate_ref, r_ref, w2_ref, gb_ref, on_ref, msum_ref, mask_ref, o_ref,
                ofwd_ref, st_ref, *, nblk, tb, c, nlev):
    ph = pl.program_id(2)
    jb = pl.program_id(3)
    blk = jnp.where(ph == 0, jb, nblk - 1 - jb)

    @pl.when(jb == 0)
    def _():
        st_ref[...] = jnp.zeros_like(st_ref)

    w2 = w2_ref[0, 0]
    gb = gb_ref[0, 0]

    def gate_input(rows):
        return _dot(r_ref[0, rows, :].astype(BF16), w2) + gb

    def load_chunk(rows, z):
        q = _slabs(q_ref, rows) * (GLA_DK ** -0.5)
        g = _log_sigmoid(z) * (1.0 / GLA_GATE_NORM)
        return q, _slabs(k_ref, rows), _slabs(v_ref, rows), g

    _scan_both_directions(ph, blk, tb, gate_input, load_chunk, ofwd_ref, o_ref, on_ref, gate_ref,
                          msum_ref=msum_ref, mask_ref=mask_ref, st_ref=st_ref, nch=tb // c, c=c, nlev=nlev,
                          state_first=True)


def _gla_scan(u, r, w2p, gbias, onorm, b, t):
    tb, c = min(SCAN_BLOCK, t), CHUNK
    nblk = t // tb
    msum, mask, nlev = _scan_consts(c)
    nh = GLA_HEADS
    ks, vs = GLA_DK // LANE, GLA_DV // LANE
    vblk0 = 2 * nh * ks // vs

    def blk_of(ph, jb):
        return jnp.where(ph == 0, jb, nblk - 1 - jb)

    def rows_of(bi, ph, jb):
        return bi * nblk + blk_of(ph, jb)

    def gate_rows(bi, ph, jb):
        return bi * nblk + jnp.where(ph == 0, nblk - 1, nblk - 1 - jb)

    return pl.pallas_call(
        functools.partial(_gla_kernel, nblk=nblk, tb=tb, c=c, nlev=nlev),
        grid=(b, nh, 2, nblk),
        in_specs=[
            pl.BlockSpec((ks, tb, LANE), lambda bi, h, ph, jb: (h, rows_of(bi, ph, jb), 0)),
            pl.BlockSpec((ks, tb, LANE), lambda bi, h, ph, jb: (nh + h, rows_of(bi, ph, jb), 0)),
            pl.BlockSpec((vs, tb, LANE), lambda bi, h, ph, jb: (vblk0 + h, rows_of(bi, ph, jb), 0)),
            pl.BlockSpec((vs, tb, LANE), lambda bi, h, ph, jb: (vblk0 + nh + h, gate_rows(bi, ph, jb), 0)),
            pl.BlockSpec((1, tb, GLA_R_PAD), lambda bi, h, ph, jb: (bi, blk_of(ph, jb), 0)),
            pl.BlockSpec((1, 1, GLA_R_PAD, GLA_DK), lambda bi, h, ph, jb: (ph, h, 0, 0)),
            pl.BlockSpec((1, 1, 1, GLA_DK), lambda bi, h, ph, jb: (ph, h, 0, 0)),
            pl.BlockSpec((1, GLA_DV), lambda bi, h, ph, jb: (0, 0)),
            pl.BlockSpec((1,) + msum.shape[1:], lambda bi, h, ph, jb: (ph, 0, 0)),
            pl.BlockSpec((1,) + mask.shape[1:], lambda bi, h, ph, jb: (ph, 0, 0, 0)),
        ],
        out_specs=pl.BlockSpec(
            (1, tb, GLA_DV), lambda bi, h, ph, jb: (bi, jnp.where(ph == 0, nblk - 1, nblk - 1 - jb), h)),
        out_shape=jax.ShapeDtypeStruct((b, t, nh * GLA_DV), BF16),
        scratch_shapes=[pltpu.VMEM((t, GLA_DV), F32), pltpu.VMEM((GLA_DV, GLA_DK), F32)],
        compiler_params=_cparams(("arbitrary",) * 4),
        name="gla_scan",
    )(u, u, u, u, r, w2p, gbias, onorm, msum, mask)


def _rope128(x, cos, sin):
    lane = lax.broadcasted_iota(jnp.int32, x.shape, 1)
    half = MLA_ROPE // 2
    swapped = jnp.where(lane < half, pltpu.roll(x, LANE - half, 1), pltpu.roll(x, half, 1))
    return x * cos + swapped * sin


def _mla_prep_kernel(c_ref, cos_ref, sin_ref, qag_ref, kvag_ref, wq_ref, wk_ref, wv_ref, gq_ref, gk_ref,
                     q_ref, k_ref, v_ref):
    c = c_ref[...]
    cq = c[:, :Q_LORA]
    ckv = c[:, Q_LORA:Q_LORA + KV_LORA]
    kpe = c[:, Q_LORA + KV_LORA:]
    cqn = (cq * lax.rsqrt(jnp.mean(cq * cq, axis=-1, keepdims=True) + EPS) * qag_ref[...]).astype(BF16)
    ckvn = (ckv * lax.rsqrt(jnp.mean(ckv * ckv, axis=-1, keepdims=True) + EPS) * kvag_ref[...]).astype(BF16)
    cos = cos_ref[...]
    sin = sin_ref[...]
    gq = gq_ref[...]
    gk = gk_ref[...]
    q_raw = _dot(cqn, wq_ref[...])
    kn_raw = _dot(ckvn, wk_ref[...])
    v_ref[...] = _dot(ckvn, wv_ref[...]).astype(BF16)
    kpe_ss = jnp.sum(kpe * kpe, axis=-1, keepdims=True)
    kpe_rot = _rope128(kpe * gk[:, LANE:], cos, sin)
    for h in range(MLA_HEADS):
        lo = h * MLA_QK_PAD
        qn = q_raw[:, lo:lo + LANE]
        qr = q_raw[:, lo + LANE:lo + 2 * LANE]
        ss = jnp.sum(qn * qn, axis=-1, keepdims=True) + jnp.sum(qr * qr, axis=-1, keepdims=True)
        rinv = lax.rsqrt(ss * (1.0 / MLA_QK) + EPS) * ATTN_Q_SCALE
        q_ref[:, lo:lo + LANE] = (qn * rinv * gq[:, :LANE]).astype(BF16)
        q_ref[:, lo + LANE:lo + 2 * LANE] = _rope128(qr * rinv * gq[:, LANE:], cos, sin).astype(BF16)
        kn = kn_raw[:, h * LANE:(h + 1) * LANE]
        ssk = jnp.sum(kn * kn, axis=-1, keepdims=True) + kpe_ss
        rinvk = lax.rsqrt(ssk * (1.0 / MLA_QK) + EPS)
        k_ref[:, lo:lo + LANE] = (kn * rinvk * gk[:, :LANE]).astype(BF16)
        k_ref[:, lo + LANE:lo + 2 * LANE] = (kpe_rot * rinvk).astype(BF16)


def _mla_prep(c, cos, sin, qag, kvag, wq, wk, wv, gq, gk, t):
    n = c.shape[0]
    tm = min(256, t)
    tps = t // tm
    full = lambda a: pl.BlockSpec(a.shape, lambda i: (0,) * a.ndim)
    return pl.pallas_call(
        _mla_prep_kernel,
        grid=(n // tm,),
        in_specs=[
            pl.BlockSpec((tm, MLA_C_PAD), lambda i: (i, 0)),
            pl.BlockSpec((tm, LANE), lambda i: (i % tps, 0)),
            pl.BlockSpec((tm, LANE), lambda i: (i % tps, 0)),
            full(qag), full(kvag), full(wq), full(wk), full(wv), full(gq), full(gk),
        ],
        out_specs=[
            pl.BlockSpec((tm, MLA_HEADS * MLA_QK_PAD), lambda i: (i, 0)),
            pl.BlockSpec((tm, MLA_HEADS * MLA_QK_PAD), lambda i: (i, 0)),
            pl.BlockSpec((tm, MLA_HEADS * MLA_V), lambda i: (i, 0)),
        ],
        out_shape=[
            jax.ShapeDtypeStruct((n, MLA_HEADS * MLA_QK_PAD), BF16),
            jax.ShapeDtypeStruct((n, MLA_HEADS * MLA_QK_PAD), BF16),
            jax.ShapeDtypeStruct((n, MLA_HEADS * MLA_V), BF16),
        ],
        compiler_params=_cparams(("arbitrary",)),
        name="mla_prep",
    )(c, cos, sin, qag, kvag, wq, wk, wv, gq, gk)


def _attn_kernel(q_ref, k_ref, v_ref, o_ref, *, tk, nk, nsplit):
    tq = q_ref.shape[1]
    th = tq // nsplit
    q = q_ref[0]

    def scores(ci):
        s = _dot_nt(q, k_ref[0, ci * tk:(ci + 1) * tk, :])
        return [s[i * th:(i + 1) * th, :] for i in range(nsplit)]

    def update(ci, s_list, state):
        vc = v_ref[0, ci * tk:(ci + 1) * tk, :]
        out = []
        for s, (m, l, acc) in zip(s_list, state):
            m_new = jnp.maximum(m, jnp.max(s, axis=-1, keepdims=True))
            alpha = jnp.exp2(m - m_new)
            p = jnp.exp2(s - m_new)
            l = alpha * l + jnp.sum(p, axis=-1, keepdims=True)
            acc = alpha * acc + _dot(p.astype(BF16), vc)
            out.append((m_new, l, acc))
        return out

    state = [(jnp.full((th, 1), -jnp.inf, F32), jnp.zeros((th, 1), F32), jnp.zeros((th, MLA_V), F32))
             for _ in range(nsplit)]
    s_cur = scores(0)
    for ci in range(nk):
        s_next = scores(ci + 1) if ci + 1 < nk else None
        state = update(ci, s_cur, state)
        s_cur = s_next
    for i, (_, l, acc) in enumerate(state):
        o_ref[0, i * th:(i + 1) * th, :] = (acc / l).astype(BF16)


def _attention(q, k, v, b, t):
    tq, tk = min(ATTN_TQ, t), min(ATTN_TK, t)
    return pl.pallas_call(
        functools.partial(_attn_kernel, tk=tk, nk=t // tk, nsplit=ATTN_SPLIT),
        grid=(b, MLA_HEADS, t // tq),
        in_specs=[
            pl.BlockSpec((1, tq, MLA_QK_PAD), lambda bi, h, i: (bi, i, h)),
            pl.BlockSpec((1, t, MLA_QK_PAD), lambda bi, h, i: (bi, 0, h)),
            pl.BlockSpec((1, t, MLA_V), lambda bi, h, i: (bi, 0, h)),
        ],
        out_specs=pl.BlockSpec((1, tq, MLA_V), lambda bi, h, i: (bi, i, h)),
        out_shape=jax.ShapeDtypeStruct((b, t, MLA_HEADS * MLA_V), BF16),
        compiler_params=_cparams(("arbitrary",) * 3),
        name="mla_attention",
    )(q, k, v)


def _rope_tables(t):
    half = MLA_ROPE // 2
    inv_freq = ROPE_THETA ** (-jnp.arange(half, dtype=F32) / half)
    ang = jnp.arange(t, dtype=jnp.int32).astype(F32)[:, None] * inv_freq[None, :]
    cos, sin = jnp.cos(ang), jnp.sin(ang)
    zeros = jnp.zeros((t, LANE - MLA_ROPE), F32)
    return (jnp.concatenate([cos, cos, zeros], axis=-1), jnp.concatenate([-sin, sin, zeros], axis=-1))


def _pad_cols(a, width):
    return jnp.pad(a, [(0, 0)] * (a.ndim - 1) + [(0, width - a.shape[-1])])


def _tile_cols(w, tn):
    k, n = w.shape[-2:]
    lead = w.shape[:-2]
    w = w.reshape(lead + (k, n // tn, tn))
    return jnp.swapaxes(w, -3, -2)


def _prep_even(e, w):
    ev_in = w['ev_w_in'][e]
    w_hg = _tile_cols(ev_in[:, :5 * HG_WIDTH].astype(BF16), INPROJ_TILE)
    w_mla = _pad_cols(ev_in[:, 5 * HG_WIDTH:], MLA_C_PAD).astype(BF16)
    uq = w['mla_w_uq'][e].reshape(Q_LORA, MLA_HEADS, MLA_QK)
    wq = _pad_cols(uq, MLA_QK_PAD).reshape(Q_LORA, MLA_HEADS * MLA_QK_PAD).astype(BF16)
    ukv = w['mla_w_ukv'][e].reshape(KV_LORA, MLA_HEADS, MLA_NOPE + MLA_V)
    wk = ukv[:, :, :MLA_NOPE].reshape(KV_LORA, MLA_HEADS * MLA_NOPE).astype(BF16)
    wv = ukv[:, :, MLA_NOPE:].reshape(KV_LORA, MLA_HEADS * MLA_V).astype(BF16)
    p = jax.nn.softmax(w['hgrn_lb'].astype(F32), axis=1)
    lb = jnp.cumsum(p, axis=1)
    lb = (lb - lb[:, :1])[:, e].reshape(2, HG_HEADS, 1, HG_D)
    lbp = jnp.concatenate([jnp.log(lb), jnp.log1p(-lb), 1.0 - lb, jnp.zeros((2, HG_HEADS, 5, HG_D), F32)], axis=2)
    w_out = w['ev_w_out'][e].astype(BF16)
    return dict(
        w_hg=w_hg, w_mla=w_mla, wq=wq, wk=wk, wv=wv, lbp=lbp,
        hg_onorm=w['hgrn_onorm_g'][e].reshape(1, HG_D),
        qag=w['mla_qa_norm_g'][e].reshape(1, Q_LORA), kvag=w['mla_kva_norm_g'][e].reshape(1, KV_LORA),
        gq=_pad_cols(w['mla_qn_g'][e].reshape(1, MLA_QK), MLA_QK_PAD),
        gk=_pad_cols(w['mla_kn_g'][e].reshape(1, MLA_QK), MLA_QK_PAD),
        w_out_hg=w_out[:HG_WIDTH], w_out_mla=w_out[HG_WIDTH:],
    )


def _prep_odd(e, w):
    od_in = w['od_w_in'][e]
    w_main = _tile_cols(od_in[:, :GLA_MAIN].astype(BF16), INPROJ_TILE)
    w_r = _pad_cols(od_in[:, GLA_MAIN:], GLA_R_PAD).astype(BF16)
    w2 = w['gla_gk_w2'][e].reshape(2, GLA_RANK, GLA_HEADS, GLA_DK).transpose(0, 2, 1, 3)
    w2p = jnp.zeros((2, GLA_HEADS, GLA_R_PAD, GLA_DK), F32)
    w2p = w2p.at[0, :, :GLA_RANK].set(w2[0]).at[1, :, GLA_RANK:2 * GLA_RANK].set(w2[1]).astype(BF16)
    gbias = w['gla_gk_b'][e].reshape(2, GLA_HEADS, 1, GLA_DK)
    return dict(w_main=w_main, w_r=w_r, w2p=w2p, gbias=gbias,
                onorm=w['gla_onorm_g'][e].reshape(1, GLA_DV), w_out=w['od_w_out'][e].astype(BF16))


def _trunk(x3, boff, nb_total, table, norm_g, w13, w2, evens, odds):
    b, t, _ = x3.shape
    n = b * t
    x = x3.reshape(n, D_MODEL)
    mod = _Mod(table, norm_g, nb_total, boff, t)
    cos, sin = _rope_tables(t)
    for layer in range(DEPTH):
        x = _ffn(x, mod, layer, 0, 0, w13, w2)
        e = layer // 2
        if layer % 2 == 0:
            p = evens[e]
            u_hg, c_mla = _inproj(x, mod, layer, p['w_hg'], p['w_mla'])
            o_hg = _hgrn_scan(u_hg, p['lbp'], p['hg_onorm'], b, t)
            q, k, v = _mla_prep(c_mla, cos, sin, p['qag'], p['kvag'], p['wq'], p['wk'], p['wv'], p['gq'], p['gk'], t)
            o_mla = _attention(q.reshape(b, t, -1), k.reshape(b, t, -1), v.reshape(b, t, -1), b, t)
            x = _outproj(x, mod, layer, [o_hg.reshape(n, HG_WIDTH), o_mla.reshape(n, MLA_HEADS * MLA_V)],
                         [p['w_out_hg'], p['w_out_mla']])
        else:
            p = odds[e]
            u, r = _inproj(x, mod, layer, p['w_main'], p['w_r'])
            o = _gla_scan(u, r.reshape(b, t, GLA_R_PAD), p['w2p'], p['gbias'], p['onorm'], b, t)
            x = _outproj(x, mod, layer, [o.reshape(n, GLA_HEADS * GLA_DV)], [p['w_out']])
        x = _ffn(x, mod, layer, 2, 1, w13, w2)
    return x.reshape(b, t, D_MODEL)


def kernel(x_prompt, x_sample, c_prompt, c_sample, ada_w, ada_b, norm_g, ffn_w13, ffn_w2, ev_w_in, ev_w_out, hgrn_lb, hgrn_onorm_g, mla_qa_norm_g, mla_w_uq, mla_kva_norm_g, mla_w_ukv, mla_qn_g, mla_kn_g, od_w_in, od_w_out, gla_gk_w2, gla_gk_b, gla_onorm_g):
    w = dict(ev_w_in=ev_w_in, ev_w_out=ev_w_out, hgrn_lb=hgrn_lb, hgrn_onorm_g=hgrn_onorm_g,
             mla_qa_norm_g=mla_qa_norm_g, mla_w_uq=mla_w_uq, mla_kva_norm_g=mla_kva_norm_g,
             mla_w_ukv=mla_w_ukv, mla_qn_g=mla_qn_g, mla_kn_g=mla_kn_g, od_w_in=od_w_in,
             od_w_out=od_w_out, gla_gk_w2=gla_gk_w2, gla_gk_b=gla_gk_b, gla_onorm_g=gla_onorm_g)
    bp, bs = x_prompt.shape[0], x_sample.shape[0]
    nb_total = bp + bs
    c_all = jnp.concatenate([c_prompt, c_sample], axis=0)
    mod = _ada_mod(c_all, ada_w, ada_b)
    table = mod.reshape(DEPTH * nb_total * 3 * N_SUB, 1, D_MODEL)
    ng = norm_g.reshape(DEPTH * N_SUB, 1, D_MODEL)
    w13 = ffn_w13.astype(BF16)
    w2 = ffn_w2.astype(BF16)
    evens = [_prep_even(e, w) for e in range((DEPTH + 1) // 2)]
    odds = [_prep_odd(e, w) for e in range(DEPTH // 2)]
    y_prompt = _trunk(x_prompt, 0, nb_total, table, ng, w13, w2, evens, odds)
    y_sample = _trunk(x_sample, bp, nb_total, table, ng, w13, w2, evens, odds)
    return (y_prompt, y_sample)
```

```python
import functools
import math

import numpy as np
import jax
import jax.numpy as jnp
from jax import lax
from jax.experimental import pallas as pl
from jax.experimental.pallas import tpu as pltpu

F32 = jnp.float32
BF16 = jnp.bfloat16

D_MODEL = 2048
DEPTH = 4
N_SUB = 3
EPS = 1e-6
FFN_DIM = 5632

HG_HEADS = 8
HG_D = 128
HG_WIDTH = HG_HEADS * HG_D
HG_HEADS_PER_STEP = 2

MLA_HEADS = 8
MLA_NOPE = 128
MLA_ROPE = 64
MLA_QK = MLA_NOPE + MLA_ROPE
MLA_V = 128
MLA_QK_PAD = 256
Q_LORA = 512
KV_LORA = 512
MLA_C_PAD = 1152
ROPE_THETA = 10000.0

GLA_HEADS = 4
GLA_DK = 256
GLA_DV = 512
GLA_RANK = 16
GLA_GATE_NORM = 16.0
GLA_MAIN = 2 * GLA_HEADS * GLA_DK + 2 * GLA_HEADS * GLA_DV
GLA_R_PAD = 128

CHUNK = 64
LANE = 128
VMEM_LIMIT = 56 * 1024 * 1024

TOKEN_TILE = 512
FFN_TOKEN_TILE = 1024
FFN_TILE = 512
INPROJ_TOKEN_TILE = 1024
INPROJ_TILE = 512
ROW_GROUPS = 2
SCAN_BLOCK = 1024
ATTN_TQ = 512
ATTN_SPLIT = 2
ATTN_TK = 1024
LOG2E = math.log2(math.e)
ATTN_Q_SCALE = MLA_QK ** -0.5 * LOG2E


def _cparams(sem):
    return pltpu.CompilerParams(dimension_semantics=sem, vmem_limit_bytes=VMEM_LIMIT)


def _silu(x):
    return x * jax.nn.sigmoid(x)


def _log_sigmoid(z):
    return jnp.minimum(z, 0.0) - jnp.log(1.0 + jnp.exp(-jnp.abs(z)))


def _adaln(x, g, scale, shift):
    ms = jnp.mean(x * x, axis=-1, keepdims=True)
    y = x * lax.rsqrt(ms + EPS) * g
    return y * (1.0 + scale) + shift


def _dot(a, b):
    return jnp.dot(a, b, preferred_element_type=F32)


def _dot_nt(a, b):
    return lax.dot_general(a, b, (((1,), (1,)), ((), ())), preferred_element_type=F32)


def _dot_tn(a, b):
    return lax.dot_general(a, b, (((0,), (0,)), ((), ())), preferred_element_type=F32)


def _ada_kernel(c_ref, w_ref, b_ref, o_ref):
    cond = _silu(c_ref[...]).astype(BF16)
    o_ref[0] = _dot(cond, w_ref[0].astype(BF16)) + b_ref[0]


def _ada_mod(c_all, ada_w, ada_b):
    nb = c_all.shape[0]
    n_out = ada_w.shape[-1]
    tn = 1024
    return pl.pallas_call(
        _ada_kernel,
        grid=(DEPTH, n_out // tn),
        in_specs=[
            pl.BlockSpec((nb, D_MODEL), lambda l, j: (0, 0)),
            pl.BlockSpec((1, D_MODEL, tn), lambda l, j: (l, 0, j)),
            pl.BlockSpec((1, 1, tn), lambda l, j: (l, 0, j)),
        ],
        out_specs=pl.BlockSpec((1, nb, tn), lambda l, j: (l, 0, j)),
        out_shape=jax.ShapeDtypeStruct((DEPTH, nb, n_out), F32),
        compiler_params=_cparams(("arbitrary", "arbitrary")),
        name="ada_mod",
    )(c_all, ada_w, ada_b.reshape(DEPTH, 1, n_out))


class _Mod:
    def __init__(self, table, norm_g, nb_total, boff, seq_len):
        self.table = table
        self.norm_g = norm_g
        self.nb_total = nb_total
        self.boff = boff
        self.seq_len = seq_len

    def spec(self, layer, sub, kind, tile):
        nb, boff, tps = self.nb_total, self.boff, self.seq_len // tile
        return pl.BlockSpec(
            (1, 1, D_MODEL),
            lambda i, *_: (((layer * nb + boff + i // tps) * 9 + sub * 3 + kind), 0, 0))

    def norm_spec(self, layer, sub):
        return pl.BlockSpec((1, 1, D_MODEL), lambda i, *_: (layer * N_SUB + sub, 0, 0))


def _ffn_kernel(x_ref, sh_ref, sc_ref, gt_ref, ng_ref, w1_ref, w3_ref, w2_ref, o_ref, h_ref, *, nj, ngroups):
    acc_ref = o_ref
    j = pl.program_id(1)
    rg = x_ref.shape[0] // ngroups

    def body(first, last):
        w1, w3, w2 = w1_ref[...], w3_ref[...], w2_ref[...]
        ups = []
        for g in range(ngroups):
            rows = slice(g * rg, (g + 1) * rg)
            if first:
                h = _adaln(x_ref[rows, :], ng_ref[0], sc_ref[0], sh_ref[0]).astype(BF16)
                h_ref[rows, :] = h
            else:
                h = h_ref[rows, :]
            ups.append((_dot(h, w1), _dot(h, w3)))
        for g, (a, u) in enumerate(ups):
            rows = slice(g * rg, (g + 1) * rg)
            down = _dot((_silu(a) * u).astype(BF16), w2)
            acc = down if first else acc_ref[rows, :] + down
            if last:
                o_ref[rows, :] = x_ref[rows, :] + (0.5 * gt_ref[0]) * acc
            else:
                acc_ref[rows, :] = acc

    pl.when(j == 0)(lambda: body(True, False))
    pl.when(jnp.logical_and(j > 0, j < nj - 1))(lambda: body(False, False))
    pl.when(j == nj - 1)(lambda: body(False, True))


def _ffn(x, mod, layer, sub, which, w13, w2):
    n = x.shape[0]
    tm, tf = min(FFN_TOKEN_TILE, mod.seq_len), FFN_TILE
    nj = FFN_DIM // tf
    return pl.pallas_call(
        functools.partial(_ffn_kernel, nj=nj, ngroups=ROW_GROUPS),
        grid=(n // tm, nj),
        in_specs=[
            pl.BlockSpec((tm, D_MODEL), lambda i, j: (i, 0)),
            mod.spec(layer, sub, 0, tm), mod.spec(layer, sub, 1, tm), mod.spec(layer, sub, 2, tm),
            mod.norm_spec(layer, sub),
            pl.BlockSpec((None, None, D_MODEL, tf), lambda i, j: (layer, which, 0, j)),
            pl.BlockSpec((None, None, D_MODEL, tf), lambda i, j: (layer, which, 0, j + nj)),
            pl.BlockSpec((None, None, tf, D_MODEL), lambda i, j: (layer, which, j, 0)),
        ],
        out_specs=pl.BlockSpec((tm, D_MODEL), lambda i, j: (i, 0)),
        out_shape=jax.ShapeDtypeStruct((n, D_MODEL), F32),
        scratch_shapes=[pltpu.VMEM((tm, D_MODEL), BF16)],
        compiler_params=_cparams(("arbitrary", "arbitrary")),
        name="ffn",
    )(x, mod.table, mod.table, mod.table, mod.norm_g, w13, w13, w2)


def _inproj_kernel(x_ref, sh_ref, sc_ref, ng_ref, wm_ref, ws_ref, om_ref, os_ref, h_ref, *, ngroups):
    j = pl.program_id(1)
    rg = x_ref.shape[0] // ngroups

    def store_main(rows, res):
        for s in range(om_ref.shape[0]):
            om_ref[s, rows, :] = res[:, s * LANE:(s + 1) * LANE]

    @pl.when(j == 0)
    def _():
        for g in range(ngroups):
            rows = slice(g * rg, (g + 1) * rg)
            h = _adaln(x_ref[rows, :], ng_ref[0], sc_ref[0], sh_ref[0]).astype(BF16)
            h_ref[rows, :] = h
            store_main(rows, _dot(h, wm_ref[...]))
            os_ref[rows, :] = _dot(h, ws_ref[...])

    @pl.when(j > 0)
    def _():
        store_main(slice(None), _dot(h_ref[...], wm_ref[...]))


def _inproj(x, mod, layer, w_main, w_small):
    n = x.shape[0]
    tm = min(INPROJ_TOKEN_TILE, mod.seq_len)
    ntiles, _, tn = w_main.shape
    n_main, n_small = ntiles * tn, w_small.shape[1]
    return pl.pallas_call(
        functools.partial(_inproj_kernel, ngroups=ROW_GROUPS),
        grid=(n // tm, ntiles),
        in_specs=[
            pl.BlockSpec((tm, D_MODEL), lambda i, j: (i, 0)),
            mod.spec(layer, 1, 0, tm), mod.spec(layer, 1, 1, tm),
            mod.norm_spec(layer, 1),
            pl.BlockSpec((None, D_MODEL, tn), lambda i, j: (j, 0, 0)),
            pl.BlockSpec((D_MODEL, n_small), lambda i, j: (0, 0)),
        ],
        out_specs=[
            pl.BlockSpec((tn // LANE, tm, LANE), lambda i, j: (j, i, 0)),
            pl.BlockSpec((tm, n_small), lambda i, j: (i, 0)),
        ],
        out_shape=[jax.ShapeDtypeStruct((n_main // LANE, n, LANE), F32), jax.ShapeDtypeStruct((n, n_small), F32)],
        scratch_shapes=[pltpu.VMEM((tm, D_MODEL), BF16)],
        compiler_params=_cparams(("arbitrary", "arbitrary")),
        name="inproj",
    )(x, mod.table, mod.table, mod.norm_g, w_main, w_small)


def _outproj_kernel(x_ref, gt_ref, *refs):
    o_ref = refs[-1]
    npair = (len(refs) - 1) // 2
    y = _dot(refs[0][...], refs[npair][...])
    for p in range(1, npair):
        y = y + _dot(refs[p][...], refs[npair + p][...])
    o_ref[...] = x_ref[...] + gt_ref[0] * y


def _outproj(x, mod, layer, mixes, ws):
    n = x.shape[0]
    tm = TOKEN_TILE
    in_specs = [pl.BlockSpec((tm, D_MODEL), lambda i: (i, 0)), mod.spec(layer, 1, 2, tm)]
    in_specs += [pl.BlockSpec((tm, m.shape[1]), lambda i: (i, 0)) for m in mixes]
    in_specs += [pl.BlockSpec(w.shape, lambda i: (0, 0)) for w in ws]
    return pl.pallas_call(
        _outproj_kernel,
        grid=(n // tm,),
        in_specs=in_specs,
        out_specs=pl.BlockSpec((tm, D_MODEL), lambda i: (i, 0)),
        out_shape=jax.ShapeDtypeStruct((n, D_MODEL), F32),
        compiler_params=_cparams(("arbitrary",)),
        name="outproj",
    )(x, mod.table, *mixes, *ws)


def _scan_consts(c):
    nlev = int(math.log2(c))
    t = np.arange(c)
    row, col = t[:, None], t[None, :]
    blocks = [col <= row]
    masks = [np.eye(c, dtype=bool)]
    for lev in range(nlev):
        s = 1 << lev
        blk = t // s
        odd = (blk % 2) == 1
        bstart = (blk * s)[:, None]
        bend = bstart + s - 1
        as_query = (col >= bstart) & (col <= row)
        as_key = (col > row) & (col <= bend)
        blocks.append(np.where(odd[:, None], as_query, as_key))
        masks.append(odd[:, None] & (~odd[None, :]) & ((row // (2 * s)) == (col // (2 * s))))
    ones = np.ones((8, c), dtype=bool)
    m_f = np.concatenate(blocks + [ones], axis=0)
    m_b = np.concatenate([b[::-1, ::-1] for b in blocks] + [ones], axis=0)
    msum = jnp.asarray(np.stack([np.tile(m_f, (1, 2)), np.tile(m_b, (1, 2))]).astype(np.float32), dtype=BF16)
    mask = jnp.asarray(np.stack([np.stack(masks), np.stack([m[::-1, ::-1] for m in masks])]).astype(np.float32))
    return msum, mask, nlev


def _slabs(ref, rows):
    parts = [ref[s, rows, :] for s in range(ref.shape[0])]
    return parts[0] if len(parts) == 1 else jnp.concatenate(parts, axis=-1)


def _split2(g):
    hi = g.astype(BF16)
    lo = (g - hi.astype(F32)).astype(BF16)
    return hi, lo


def _scan_block(backward, gate_input, load_chunk, emit, msum_ref, mask_ref, st_ref, *, nch, c, nlev, state_first,
                nheads):
    msum = msum_ref[0]
    masks = [mask_ref[0, i] > 0.5 for i in range(nlev + 1)]
    starts = [((nch - 1 - ci) if backward else ci) * c for ci in range(nch)]
    order = [slice(r0, r0 + c) for r0 in starts]

    def head(a, s):
        w = a.shape[-1] // nheads
        return a[:, s * w:(s + 1) * w]

    def stage1(rows, z):
        q, k, v, g = load_chunk(rows, z)
        hi, lo = _split2(g * LOG2E)
        return q, k, v.astype(BF16), _dot(msum, jnp.concatenate([hi, lo], axis=0))

    def stage2(q, k, vb, x):
        e = jnp.exp2(x)
        tot = x[(1 + nlev) * c:(1 + nlev) * c + 1]
        dtot = e[(1 + nlev) * c:(1 + nlev) * c + 1]
        qe = (q * e[0:c]).astype(BF16)
        ke = (k * jnp.exp2(tot - x[0:c])).astype(BF16)
        qb, kb = q.astype(BF16), k.astype(BF16)
        eb = e[c:(1 + nlev) * c].astype(BF16)
        att = [jnp.where(masks[0], _dot_nt(head(qb, s), head(kb, s)), 0.0) for s in range(nheads)]
        for lev in range(nlev):
            es = eb[lev * c:(lev + 1) * c]
            qs, ks = qb * es, kb * es
            att = [jnp.where(masks[1 + lev], _dot_nt(head(qs, s), head(ks, s)), att[s]) for s in range(nheads)]
        inc = [_dot_tn(head(vb, s), head(ke, s)) for s in range(nheads)]
        return qe, dtot, [a.astype(BF16) for a in att], vb, inc

    def stage3(rows, st, qe, dtot, att, vb, inc):
        o = [_dot(att[s], head(vb, s)) + _dot_nt(head(qe, s), st[s].astype(BF16)) for s in range(nheads)]
        emit(rows, o[0] if nheads == 1 else jnp.concatenate(o, axis=-1))
        return [st[s] * head(dtot, s) + inc[s] for s in range(nheads)]

    zs, s1, s2 = {}, {}, {}
    st = [st_ref[s] for s in range(nheads)]
    for step in range(nch + 3):
        if step < nch:
            zs[step] = gate_input(order[step])
        if 0 <= step - 1 < nch:
            s1[step - 1] = stage1(order[step - 1], zs.pop(step - 1))
        if state_first and 0 <= step - 3 < nch:
            st = stage3(order[step - 3], st, *s2.pop(step - 3))
        if 0 <= step - 2 < nch:
            s2[step - 2] = stage2(*s1.pop(step - 2))
        if not state_first and 0 <= step - 3 < nch:
            st = stage3(order[step - 3], st, *s2.pop(step - 3))
    for s in range(nheads):
        st_ref[s] = st[s]


def _scan_both_directions(ph, blk, tb, gate_input, load_chunk, ofwd_ref, o_ref, on_ref, gate_ref, **kw):
    c, nheads = kw["c"], kw["nheads"]

    def seq_rows(rows):
        return pl.ds(pl.multiple_of(blk * tb + rows.start, c), c)

    def emit_fwd(rows, o):
        ofwd_ref[seq_rows(rows), :] = o

    def emit_bwd(rows, o):
        o_sum = ofwd_ref[seq_rows(rows), :] + o
        gate = _silu(_slabs(gate_ref, rows))
        dv = o_sum.shape[-1] // nheads
        for s in range(nheads):
            cols = slice(s * dv, (s + 1) * dv)
            os_ = o_sum[:, cols]
            ms = jnp.mean(os_ * os_, axis=-1, keepdims=True)
            o_ref[0, rows, cols] = (os_ * lax.rsqrt(ms + EPS) * on_ref[...] * gate[:, cols]).astype(BF16)

    pl.when(ph == 0)(lambda: _scan_block(False, gate_input, load_chunk, emit_fwd, **kw))
    pl.when(ph == 1)(lambda: _scan_block(True, gate_input, load_chunk, emit_bwd, **kw))


def _hgrn_kernel(q_ref, z_ref, v_ref, gate_ref, lb_ref, on_ref, msum_ref, mask_ref, o_ref, ofwd_ref, st_ref,
                 *, nblk, tb, c, nlev):
    ph = pl.program_id(2)
    jb = pl.program_id(3)
    blk = jnp.where(ph == 0, jb, nblk - 1 - jb)

    @pl.when(jb == 0)
    def _():
        st_ref[...] = jnp.zeros_like(st_ref)

    def lb_row(r):
        return jnp.concatenate([lb_ref[0, s, r:r + 1, :] for s in range(lb_ref.shape[1])], axis=-1)

    log_lb, log_1mlb, one_mlb = lb_row(0), lb_row(1), lb_row(2)

    def gate_input(rows):
        return _slabs(z_ref, rows)

    def load_chunk(rows, z):
        q = _silu(_slabs(q_ref, rows))
        t = jnp.exp(-jnp.abs(z))
        d = 1.0 + t
        a = log_lb
        b = log_1mlb + (jnp.minimum(z, 0.0) - jnp.log(d))
        g = jnp.maximum(a, b) + jnp.log(1.0 + jnp.exp(-jnp.abs(a - b)))
        k = one_mlb * (jnp.where(z > 0.0, t, 1.0) / d)
        return q, k, _slabs(v_ref, rows), g

    _scan_both_directions(ph, blk, tb, gate_input, load_chunk, ofwd_ref, o_ref, on_ref, gate_ref,
                          msum_ref=msum_ref, mask_ref=mask_ref, st_ref=st_ref, nch=tb // c, c=c, nlev=nlev,
                          state_first=False, nheads=HG_HEADS_PER_STEP)


def _hgrn_scan(u, lbp, onorm, b, t):
    tb, c = min(SCAN_BLOCK, t), CHUNK
    nblk = t // tb
    msum, mask, nlev = _scan_consts(c)
    hp = HG_HEADS_PER_STEP
    ng = HG_HEADS // hp

    def rows_of(bi, ph, jb):
        return bi * nblk + jnp.where(ph == 0, jb, nblk - 1 - jb)

    def gate_rows(bi, ph, jb):
        return bi * nblk + jnp.where(ph == 0, nblk - 1, nblk - 1 - jb)

    return pl.pallas_call(
        functools.partial(_hgrn_kernel, nblk=nblk, tb=tb, c=c, nlev=nlev),
        grid=(b, ng, 2, nblk),
        in_specs=[
            pl.BlockSpec((hp, tb, HG_D), lambda bi, h, ph, jb: (h, rows_of(bi, ph, jb), 0)),
            pl.BlockSpec((hp, tb, HG_D), lambda bi, h, ph, jb: (ng + ng * ph + h, rows_of(bi, ph, jb), 0)),
            pl.BlockSpec((hp, tb, HG_D), lambda bi, h, ph, jb: (3 * ng + h, rows_of(bi, ph, jb), 0)),
            pl.BlockSpec((hp, tb, HG_D), lambda bi, h, ph, jb: (4 * ng + h, gate_rows(bi, ph, jb), 0)),
            pl.BlockSpec((1, hp, 8, HG_D), lambda bi, h, ph, jb: (ph, h, 0, 0)),
            pl.BlockSpec((1, HG_D), lambda bi, h, ph, jb: (0, 0)),
            pl.BlockSpec((1,) + msum.shape[1:], lambda bi, h, ph, jb: (ph, 0, 0)),
            pl.BlockSpec((1,) + mask.shape[1:], lambda bi, h, ph, jb: (ph, 0, 0, 0)),
        ],
        out_specs=pl.BlockSpec(
            (1, tb, hp * HG_D), lambda bi, h, ph, jb: (bi, jnp.where(ph == 0, nblk - 1, nblk - 1 - jb), h)),
        out_shape=jax.ShapeDtypeStruct((b, t, HG_WIDTH), BF16),
        scratch_shapes=[pltpu.VMEM((t, hp * HG_D), F32), pltpu.VMEM((hp, HG_D, HG_D), F32)],
        compiler_params=_cparams(("arbitrary",) * 4),
        name="hgrn_scan",
    )(u, u, u, u, lbp, onorm, msum, mask)


def _gla_kernel(q_ref, k_ref, v_ref, gate_ref, r_ref, w2_ref, gb_ref, on_ref, msum_ref, mask_ref, o_ref,
                ofwd_ref, st_ref, *, nblk, tb, c, nlev):
    ph = pl.program_id(2)
    jb = pl.program_id(3)
    blk = jnp.where(ph == 0, jb, nblk - 1 - jb)

    @pl.when(jb == 0)
    def _():
        st_ref[...] = jnp.zeros_like(st_ref)

    w2 = w2_ref[0, 0]
    gb = gb_ref[0, 0]

    def gate_input(rows):
        return _dot(r_ref[0, rows, :].astype(BF16), w2) + gb

    def load_chunk(rows, z):
        q = _slabs(q_ref, rows) * (GLA_DK ** -0.5)
        g = _log_sigmoid(z) * (1.0 / GLA_GATE_NORM)
        return q, _slabs(k_ref, rows), _slabs(v_ref, rows), g

    _scan_both_directions(ph, blk, tb, gate_input, load_chunk, ofwd_ref, o_ref, on_ref, gate_ref,
                          msum_ref=msum_ref, mask_ref=mask_ref, st_ref=st_ref, nch=tb // c, c=c, nlev=nlev,
                          state_first=True, nheads=1)


def _gla_scan(u, r, w2p, gbias, onorm, b, t):
    tb, c = min(SCAN_BLOCK, t), CHUNK
    nblk = t // tb
    msum, mask, nlev = _scan_consts(c)
    nh = GLA_HEADS
    ks, vs = GLA_DK // LANE, GLA_DV // LANE
    vblk0 = 2 * nh * ks // vs

    def blk_of(ph, jb):
        return jnp.where(ph == 0, jb, nblk - 1 - jb)

    def rows_of(bi, ph, jb):
        return bi * nblk + blk_of(ph, jb)

    def gate_rows(bi, ph, jb):
        return bi * nblk + jnp.where(ph == 0, nblk - 1, nblk - 1 - jb)

    return pl.pallas_call(
        functools.partial(_gla_kernel, nblk=nblk, tb=tb, c=c, nlev=nlev),
        grid=(b, nh, 2, nblk),
        in_specs=[
            pl.BlockSpec((ks, tb, LANE), lambda bi, h, ph, jb: (h, rows_of(bi, ph, jb), 0)),
            pl.BlockSpec((ks, tb, LANE), lambda bi, h, ph, jb: (nh + h, rows_of(bi, ph, jb), 0)),
            pl.BlockSpec((vs, tb, LANE), lambda bi, h, ph, jb: (vblk0 + h, rows_of(bi, ph, jb), 0)),
            pl.BlockSpec((vs, tb, LANE), lambda bi, h, ph, jb: (vblk0 + nh + h, gate_rows(bi, ph, jb), 0)),
            pl.BlockSpec((1, tb, GLA_R_PAD), lambda bi, h, ph, jb: (bi, blk_of(ph, jb), 0)),
            pl.BlockSpec((1, 1, GLA_R_PAD, GLA_DK), lambda bi, h, ph, jb: (ph, h, 0, 0)),
            pl.BlockSpec((1, 1, 1, GLA_DK), lambda bi, h, ph, jb: (ph, h, 0, 0)),
            pl.BlockSpec((1, GLA_DV), lambda bi, h, ph, jb: (0, 0)),
            pl.BlockSpec((1,) + msum.shape[1:], lambda bi, h, ph, jb: (ph, 0, 0)),
            pl.BlockSpec((1,) + mask.shape[1:], lambda bi, h, ph, jb: (ph, 0, 0, 0)),
        ],
        out_specs=pl.BlockSpec(
            (1, tb, GLA_DV), lambda bi, h, ph, jb: (bi, jnp.where(ph == 0, nblk - 1, nblk - 1 - jb), h)),
        out_shape=jax.ShapeDtypeStruct((b, t, nh * GLA_DV), BF16),
        scratch_shapes=[pltpu.VMEM((t, GLA_DV), F32), pltpu.VMEM((1, GLA_DV, GLA_DK), F32)],
        compiler_params=_cparams(("arbitrary",) * 4),
        name="gla_scan",
    )(u, u, u, u, r, w2p, gbias, onorm, msum, mask)


def _rope128(x, cos, sin):
    lane = lax.broadcasted_iota(jnp.int32, x.shape, 1)
    half = MLA_ROPE // 2
    swapped = jnp.where(lane < half, pltpu.roll(x, LANE - half, 1), pltpu.roll(x, half, 1))
    return x * cos + swapped * sin


def _mla_prep_kernel(c_ref, cos_ref, sin_ref, qag_ref, kvag_ref, wq_ref, wk_ref, wv_ref, gq_ref, gk_ref,
                     q_ref, k_ref, v_ref):
    c = c_ref[...]
    cq = c[:, :Q_LORA]
    ckv = c[:, Q_LORA:Q_LORA + KV_LORA]
    kpe = c[:, Q_LORA + KV_LORA:]
    cqn = (cq * lax.rsqrt(jnp.mean(cq * cq, axis=-1, keepdims=True) + EPS) * qag_ref[...]).astype(BF16)
    ckvn = (ckv * lax.rsqrt(jnp.mean(ckv * ckv, axis=-1, keepdims=True) + EPS) * kvag_ref[...]).astype(BF16)
    cos = cos_ref[...]
    sin = sin_ref[...]
    gq = gq_ref[...]
    gk = gk_ref[...]
    q_raw = _dot(cqn, wq_ref[...])
    kn_raw = _dot(ckvn, wk_ref[...])
    v_ref[...] = _dot(ckvn, wv_ref[...]).astype(BF16)
    kpe_ss = jnp.sum(kpe * kpe, axis=-1, keepdims=True)
    kpe_rot = _rope128(kpe * gk[:, LANE:], cos, sin)
    for h in range(MLA_HEADS):
        lo = h * MLA_QK_PAD
        qn = q_raw[:, lo:lo + LANE]
        qr = q_raw[:, lo + LANE:lo + 2 * LANE]
        ss = jnp.sum(qn * qn, axis=-1, keepdims=True) + jnp.sum(qr * qr, axis=-1, keepdims=True)
        rinv = lax.rsqrt(ss * (1.0 / MLA_QK) + EPS) * ATTN_Q_SCALE
        q_ref[:, lo:lo + LANE] = (qn * rinv * gq[:, :LANE]).astype(BF16)
        q_ref[:, lo + LANE:lo + 2 * LANE] = _rope128(qr * rinv * gq[:, LANE:], cos, sin).astype(BF16)
        kn = kn_raw[:, h * LANE:(h + 1) * LANE]
        ssk = jnp.sum(kn * kn, axis=-1, keepdims=True) + kpe_ss
        rinvk = lax.rsqrt(ssk * (1.0 / MLA_QK) + EPS)
        k_ref[:, lo:lo + LANE] = (kn * rinvk * gk[:, :LANE]).astype(BF16)
        k_ref[:, lo + LANE:lo + 2 * LANE] = (kpe_rot * rinvk).astype(BF16)


def _mla_prep(c, cos, sin, qag, kvag, wq, wk, wv, gq, gk, t):
    n = c.shape[0]
    tm = min(256, t)
    tps = t // tm
    full = lambda a: pl.BlockSpec(a.shape, lambda i: (0,) * a.ndim)
    return pl.pallas_call(
        _mla_prep_kernel,
        grid=(n // tm,),
        in_specs=[
            pl.BlockSpec((tm, MLA_C_PAD), lambda i: (i, 0)),
            pl.BlockSpec((tm, LANE), lambda i: (i % tps, 0)),
            pl.BlockSpec((tm, LANE), lambda i: (i % tps, 0)),
            full(qag), full(kvag), full(wq), full(wk), full(wv), full(gq), full(gk),
        ],
        out_specs=[
            pl.BlockSpec((tm, MLA_HEADS * MLA_QK_PAD), lambda i: (i, 0)),
            pl.BlockSpec((tm, MLA_HEADS * MLA_QK_PAD), lambda i: (i, 0)),
            pl.BlockSpec((tm, MLA_HEADS * MLA_V), lambda i: (i, 0)),
        ],
        out_shape=[
            jax.ShapeDtypeStruct((n, MLA_HEADS * MLA_QK_PAD), BF16),
            jax.ShapeDtypeStruct((n, MLA_HEADS * MLA_QK_PAD), BF16),
            jax.ShapeDtypeStruct((n, MLA_HEADS * MLA_V), BF16),
        ],
        compiler_params=_cparams(("arbitrary",)),
        name="mla_prep",
    )(c, cos, sin, qag, kvag, wq, wk, wv, gq, gk)


def _attn_kernel(q_ref, k_ref, v_ref, o_ref, *, tk, nk, nsplit):
    tq = q_ref.shape[1]
    th = tq // nsplit
    q = q_ref[0]

    def scores(ci):
        s = _dot_nt(q, k_ref[0, ci * tk:(ci + 1) * tk, :])
        return [s[i * th:(i + 1) * th, :] for i in range(nsplit)]

    def update(ci, s_list, state):
        vc = v_ref[0, ci * tk:(ci + 1) * tk, :]
        out = []
        for s, (m, l, acc) in zip(s_list, state):
            m_new = jnp.maximum(m, jnp.max(s, axis=-1, keepdims=True))
            alpha = jnp.exp2(m - m_new)
            p = jnp.exp2(s - m_new)
            l = alpha * l + jnp.sum(p, axis=-1, keepdims=True)
            acc = alpha * acc + _dot(p.astype(BF16), vc)
            out.append((m_new, l, acc))
        return out

    state = [(jnp.full((th, 1), -jnp.inf, F32), jnp.zeros((th, 1), F32), jnp.zeros((th, MLA_V), F32))
             for _ in range(nsplit)]
    s_cur = scores(0)
    for ci in range(nk):
        s_next = scores(ci + 1) if ci + 1 < nk else None
        state = update(ci, s_cur, state)
        s_cur = s_next
    for i, (_, l, acc) in enumerate(state):
        o_ref[0, i * th:(i + 1) * th, :] = (acc / l).astype(BF16)


def _attention(q, k, v, b, t):
    tq, tk = min(ATTN_TQ, t), min(ATTN_TK, t)
    return pl.pallas_call(
        functools.partial(_attn_kernel, tk=tk, nk=t // tk, nsplit=ATTN_SPLIT),
        grid=(b, MLA_HEADS, t // tq),
        in_specs=[
            pl.BlockSpec((1, tq, MLA_QK_PAD), lambda bi, h, i: (bi, i, h)),
            pl.BlockSpec((1, t, MLA_QK_PAD), lambda bi, h, i: (bi, 0, h)),
            pl.BlockSpec((1, t, MLA_V), lambda bi, h, i: (bi, 0, h)),
        ],
        out_specs=pl.BlockSpec((1, tq, MLA_V), lambda bi, h, i: (bi, i, h)),
        out_shape=jax.ShapeDtypeStruct((b, t, MLA_HEADS * MLA_V), BF16),
        compiler_params=_cparams(("arbitrary",) * 3),
        name="mla_attention",
    )(q, k, v)


def _rope_tables(t):
    half = MLA_ROPE // 2
    inv_freq = ROPE_THETA ** (-jnp.arange(half, dtype=F32) / half)
    ang = jnp.arange(t, dtype=jnp.int32).astype(F32)[:, None] * inv_freq[None, :]
    cos, sin = jnp.cos(ang), jnp.sin(ang)
    zeros = jnp.zeros((t, LANE - MLA_ROPE), F32)
    return (jnp.concatenate([cos, cos, zeros], axis=-1), jnp.concatenate([-sin, sin, zeros], axis=-1))


def _pad_cols(a, width):
    return jnp.pad(a, [(0, 0)] * (a.ndim - 1) + [(0, width - a.shape[-1])])


def _tile_cols(w, tn):
    k, n = w.shape[-2:]
    lead = w.shape[:-2]
    w = w.reshape(lead + (k, n // tn, tn))
    return jnp.swapaxes(w, -3, -2)


def _prep_even(e, w):
    ev_in = w['ev_w_in'][e]
    w_hg = _tile_cols(ev_in[:, :5 * HG_WIDTH].astype(BF16), INPROJ_TILE)
    w_mla = _pad_cols(ev_in[:, 5 * HG_WIDTH:], MLA_C_PAD).astype(BF16)
    uq = w['mla_w_uq'][e].reshape(Q_LORA, MLA_HEADS, MLA_QK)
    wq = _pad_cols(uq, MLA_QK_PAD).reshape(Q_LORA, MLA_HEADS * MLA_QK_PAD).astype(BF16)
    ukv = w['mla_w_ukv'][e].reshape(KV_LORA, MLA_HEADS, MLA_NOPE + MLA_V)
    wk = ukv[:, :, :MLA_NOPE].reshape(KV_LORA, MLA_HEADS * MLA_NOPE).astype(BF16)
    wv = ukv[:, :, MLA_NOPE:].reshape(KV_LORA, MLA_HEADS * MLA_V).astype(BF16)
    p = jax.nn.softmax(w['hgrn_lb'].astype(F32), axis=1)
    lb = jnp.cumsum(p, axis=1)
    lb = (lb - lb[:, :1])[:, e].reshape(2, HG_HEADS, 1, HG_D)
    lbp = jnp.concatenate([jnp.log(lb), jnp.log1p(-lb), 1.0 - lb, jnp.zeros((2, HG_HEADS, 5, HG_D), F32)], axis=2)
    w_out = w['ev_w_out'][e].astype(BF16)
    return dict(
        w_hg=w_hg, w_mla=w_mla, wq=wq, wk=wk, wv=wv, lbp=lbp,
        hg_onorm=w['hgrn_onorm_g'][e].reshape(1, HG_D),
        qag=w['mla_qa_norm_g'][e].reshape(1, Q_LORA), kvag=w['mla_kva_norm_g'][e].reshape(1, KV_LORA),
        gq=_pad_cols(w['mla_qn_g'][e].reshape(1, MLA_QK), MLA_QK_PAD),
        gk=_pad_cols(w['mla_kn_g'][e].reshape(1, MLA_QK), MLA_QK_PAD),
        w_out_hg=w_out[:HG_WIDTH], w_out_mla=w_out[HG_WIDTH:],
    )


def _prep_odd(e, w):
    od_in = w['od_w_in'][e]
    w_main = _tile_cols(od_in[:, :GLA_MAIN].astype(BF16), INPROJ_TILE)
    w_r = _pad_cols(od_in[:, GLA_MAIN:], GLA_R_PAD).astype(BF16)
    w2 = w['gla_gk_w2'][e].reshape(2, GLA_RANK, GLA_HEADS, GLA_DK).transpose(0, 2, 1, 3)
    w2p = jnp.zeros((2, GLA_HEADS, GLA_R_PAD, GLA_DK), F32)
    w2p = w2p.at[0, :, :GLA_RANK].set(w2[0]).at[1, :, GLA_RANK:2 * GLA_RANK].set(w2[1]).astype(BF16)
    gbias = w['gla_gk_b'][e].reshape(2, GLA_HEADS, 1, GLA_DK)
    return dict(w_main=w_main, w_r=w_r, w2p=w2p, gbias=gbias,
                onorm=w['gla_onorm_g'][e].reshape(1, GLA_DV), w_out=w['od_w_out'][e].astype(BF16))


def _trunk(x3, boff, nb_total, table, norm_g, w13, w2, evens, odds):
    b, t, _ = x3.shape
    n = b * t
    x = x3.reshape(n, D_MODEL)
    mod = _Mod(table, norm_g, nb_total, boff, t)
    cos, sin = _rope_tables(t)
    for layer in range(DEPTH):
        x = _ffn(x, mod, layer, 0, 0, w13, w2)
        e = layer // 2
        if layer % 2 == 0:
            p = evens[e]
            u_hg, c_mla = _inproj(x, mod, layer, p['w_hg'], p['w_mla'])
            o_hg = _hgrn_scan(u_hg, p['lbp'], p['hg_onorm'], b, t)
            q, k, v = _mla_prep(c_mla, cos, sin, p['qag'], p['kvag'], p['wq'], p['wk'], p['wv'], p['gq'], p['gk'], t)
            o_mla = _attention(q.reshape(b, t, -1), k.reshape(b, t, -1), v.reshape(b, t, -1), b, t)
            x = _outproj(x, mod, layer, [o_hg.reshape(n, HG_WIDTH), o_mla.reshape(n, MLA_HEADS * MLA_V)],
                         [p['w_out_hg'], p['w_out_mla']])
        else:
            p = odds[e]
            u, r = _inproj(x, mod, layer, p['w_main'], p['w_r'])
            o = _gla_scan(u, r.reshape(b, t, GLA_R_PAD), p['w2p'], p['gbias'], p['onorm'], b, t)
            x = _outproj(x, mod, layer, [o.reshape(n, GLA_HEADS * GLA_DV)], [p['w_out']])
        x = _ffn(x, mod, layer, 2, 1, w13, w2)
    return x.reshape(b, t, D_MODEL)


def kernel(x_prompt, x_sample, c_prompt, c_sample, ada_w, ada_b, norm_g, ffn_w13, ffn_w2, ev_w_in, ev_w_out, hgrn_lb, hgrn_onorm_g, mla_qa_norm_g, mla_w_uq, mla_kva_norm_g, mla_w_ukv, mla_qn_g, mla_kn_g, od_w_in, od_w_out, gla_gk_w2, gla_gk_b, gla_onorm_g):
    w = dict(ev_w_in=ev_w_in, ev_w_out=ev_w_out, hgrn_lb=hgrn_lb, hgrn_onorm_g=hgrn_onorm_g,
             mla_qa_norm_g=mla_qa_norm_g, mla_w_uq=mla_w_uq, mla_kva_norm_g=mla_kva_norm_g,
             mla_w_ukv=mla_w_ukv, mla_qn_g=mla_qn_g, mla_kn_g=mla_kn_g, od_w_in=od_w_in,
             od_w_out=od_w_out, gla_gk_w2=gla_gk_w2, gla_gk_b=gla_gk_b, gla_onorm_g=gla_onorm_g)
    bp, bs = x_prompt.shape[0], x_sample.shape[0]
    nb_total = bp + bs
    c_all = jnp.concatenate([c_prompt, c_sample], axis=0)
    mod = _ada_mod(c_all, ada_w, ada_b)
    table = mod.reshape(DEPTH * nb_total * 3 * N_SUB, 1, D_MODEL)
    ng = norm_g.reshape(DEPTH * N_SUB, 1, D_MODEL)
    w13 = ffn_w13.astype(BF16)
    w2 = ffn_w2.astype(BF16)
    evens = [_prep_even(e, w) for e in range((DEPTH + 1) // 2)]
    odds = [_prep_odd(e, w) for e in range(DEPTH // 2)]
    y_prompt = _trunk(x_prompt, 0, nb_total, table, ng, w13, w2, evens, odds)
    y_sample = _trunk(x_sample, bp, nb_total, table, ng, w13, w2, evens, odds)
    return (y_prompt, y_sample)
```

```python
import functools
import math

import numpy as np
import jax
import jax.numpy as jnp
from jax import lax
from jax.experimental import pallas as pl
from jax.experimental.pallas import tpu as pltpu

F32 = jnp.float32
BF16 = jnp.bfloat16

D_MODEL = 2048
DEPTH = 4
N_SUB = 3
EPS = 1e-6
FFN_DIM = 5632

HG_HEADS = 8
HG_D = 128
HG_WIDTH = HG_HEADS * HG_D
HG_HEADS_PER_STEP = 2

MLA_HEADS = 8
MLA_NOPE = 128
MLA_ROPE = 64
MLA_QK = MLA_NOPE + MLA_ROPE
MLA_V = 128
MLA_QK_PAD = 256
Q_LORA = 512
KV_LORA = 512
MLA_C_PAD = 1152
ROPE_THETA = 10000.0

GLA_HEADS = 4
GLA_DK = 256
GLA_DV = 512
GLA_RANK = 16
GLA_GATE_NORM = 16.0
GLA_MAIN = 2 * GLA_HEADS * GLA_DK + 2 * GLA_HEADS * GLA_DV
GLA_R_PAD = 128

CHUNK = 64
LANE = 128
VMEM_LIMIT = 56 * 1024 * 1024

TOKEN_TILE = 512
FFN_TOKEN_TILE = 1024
FFN_TILE = 512
FFN_ROW_GROUPS = 4
INPROJ_TOKEN_TILE = 1024
INPROJ_TILE = 512
ROW_GROUPS = 2
SCAN_BLOCK = 1024
ATTN_TQ = 512
ATTN_SPLIT = 2
ATTN_TK = 1024
LOG2E = math.log2(math.e)
ATTN_Q_SCALE = MLA_QK ** -0.5 * LOG2E


def _cparams(sem):
    return pltpu.CompilerParams(dimension_semantics=sem, vmem_limit_bytes=VMEM_LIMIT)


def _silu(x):
    return x * jax.nn.sigmoid(x)


def _log_sigmoid(z):
    return jnp.minimum(z, 0.0) - jnp.log(1.0 + jnp.exp(-jnp.abs(z)))


def _adaln(x, g, scale, shift):
    ms = jnp.mean(x * x, axis=-1, keepdims=True)
    y = x * lax.rsqrt(ms + EPS) * g
    return y * (1.0 + scale) + shift


def _dot(a, b):
    return jnp.dot(a, b, preferred_element_type=F32)


def _dot_nt(a, b):
    return lax.dot_general(a, b, (((1,), (1,)), ((), ())), preferred_element_type=F32)


def _dot_tn(a, b):
    return lax.dot_general(a, b, (((0,), (0,)), ((), ())), preferred_element_type=F32)


def _ada_kernel(c_ref, w_ref, b_ref, o_ref):
    cond = _silu(c_ref[...]).astype(BF16)
    o_ref[0] = _dot(cond, w_ref[0].astype(BF16)) + b_ref[0]


def _ada_mod(c_all, ada_w, ada_b):
    nb = c_all.shape[0]
    n_out = ada_w.shape[-1]
    tn = 1024
    return pl.pallas_call(
        _ada_kernel,
        grid=(DEPTH, n_out // tn),
        in_specs=[
            pl.BlockSpec((nb, D_MODEL), lambda l, j: (0, 0)),
            pl.BlockSpec((1, D_MODEL, tn), lambda l, j: (l, 0, j)),
            pl.BlockSpec((1, 1, tn), lambda l, j: (l, 0, j)),
        ],
        out_specs=pl.BlockSpec((1, nb, tn), lambda l, j: (l, 0, j)),
        out_shape=jax.ShapeDtypeStruct((DEPTH, nb, n_out), F32),
        compiler_params=_cparams(("arbitrary", "arbitrary")),
        name="ada_mod",
    )(c_all, ada_w, ada_b.reshape(DEPTH, 1, n_out))


class _Mod:
    def __init__(self, table, norm_g, nb_total, boff, seq_len):
        self.table = table
        self.norm_g = norm_g
        self.nb_total = nb_total
        self.boff = boff
        self.seq_len = seq_len

    def spec(self, layer, sub, kind, tile):
        nb, boff, tps = self.nb_total, self.boff, self.seq_len // tile
        return pl.BlockSpec(
            (1, 1, D_MODEL),
            lambda i, *_: (((layer * nb + boff + i // tps) * 9 + sub * 3 + kind), 0, 0))

    def norm_spec(self, layer, sub):
        return pl.BlockSpec((1, 1, D_MODEL), lambda i, *_: (layer * N_SUB + sub, 0, 0))


def _ffn_kernel(x_ref, sh_ref, sc_ref, gt_ref, ng_ref, w1_ref, w3_ref, w2_ref, o_ref, h_ref, *, nj, ngroups):
    acc_ref = o_ref
    j = pl.program_id(1)
    rg = x_ref.shape[0] // ngroups

    def body(first, last):
        w1, w3, w2 = w1_ref[...], w3_ref[...], w2_ref[...]
        ups = []
        for g in range(ngroups):
            rows = slice(g * rg, (g + 1) * rg)
            if first:
                h = _adaln(x_ref[rows, :], ng_ref[0], sc_ref[0], sh_ref[0]).astype(BF16)
                h_ref[rows, :] = h
            else:
                h = h_ref[rows, :]
            ups.append((_dot(h, w1), _dot(h, w3)))
        for g, (a, u) in enumerate(ups):
            rows = slice(g * rg, (g + 1) * rg)
            down = _dot((_silu(a) * u).astype(BF16), w2)
            acc = down if first else acc_ref[rows, :] + down
            if last:
                o_ref[rows, :] = x_ref[rows, :] + (0.5 * gt_ref[0]) * acc
            else:
                acc_ref[rows, :] = acc

    pl.when(j == 0)(lambda: body(True, False))
    pl.when(jnp.logical_and(j > 0, j < nj - 1))(lambda: body(False, False))
    pl.when(j == nj - 1)(lambda: body(False, True))


def _ffn(x, mod, layer, sub, which, w13, w2):
    n = x.shape[0]
    tm, tf = min(FFN_TOKEN_TILE, mod.seq_len), FFN_TILE
    nj = FFN_DIM // tf
    return pl.pallas_call(
        functools.partial(_ffn_kernel, nj=nj, ngroups=FFN_ROW_GROUPS),
        grid=(n // tm, nj),
        in_specs=[
            pl.BlockSpec((tm, D_MODEL), lambda i, j: (i, 0)),
            mod.spec(layer, sub, 0, tm), mod.spec(layer, sub, 1, tm), mod.spec(layer, sub, 2, tm),
            mod.norm_spec(layer, sub),
            pl.BlockSpec((None, None, D_MODEL, tf), lambda i, j: (layer, which, 0, j)),
            pl.BlockSpec((None, None, D_MODEL, tf), lambda i, j: (layer, which, 0, j + nj)),
            pl.BlockSpec((None, None, tf, D_MODEL), lambda i, j: (layer, which, j, 0)),
        ],
        out_specs=pl.BlockSpec((tm, D_MODEL), lambda i, j: (i, 0)),
        out_shape=jax.ShapeDtypeStruct((n, D_MODEL), F32),
        scratch_shapes=[pltpu.VMEM((tm, D_MODEL), BF16)],
        compiler_params=_cparams(("arbitrary", "arbitrary")),
        name="ffn",
    )(x, mod.table, mod.table, mod.table, mod.norm_g, w13, w13, w2)


def _inproj_kernel(x_ref, sh_ref, sc_ref, ng_ref, wm_ref, ws_ref, om_ref, os_ref, h_ref, *, ngroups):
    j = pl.program_id(1)
    rg = x_ref.shape[0] // ngroups

    def store_main(rows, res):
        for s in range(om_ref.shape[0]):
            om_ref[s, rows, :] = res[:, s * LANE:(s + 1) * LANE]

    @pl.when(j == 0)
    def _():
        for g in range(ngroups):
            rows = slice(g * rg, (g + 1) * rg)
            h = _adaln(x_ref[rows, :], ng_ref[0], sc_ref[0], sh_ref[0]).astype(BF16)
            h_ref[rows, :] = h
            store_main(rows, _dot(h, wm_ref[...]))
            os_ref[rows, :] = _dot(h, ws_ref[...])

    @pl.when(j > 0)
    def _():
        store_main(slice(None), _dot(h_ref[...], wm_ref[...]))


def _inproj(x, mod, layer, w_main, w_small):
    n = x.shape[0]
    tm = min(INPROJ_TOKEN_TILE, mod.seq_len)
    ntiles, _, tn = w_main.shape
    n_main, n_small = ntiles * tn, w_small.shape[1]
    return pl.pallas_call(
        functools.partial(_inproj_kernel, ngroups=ROW_GROUPS),
        grid=(n // tm, ntiles),
        in_specs=[
            pl.BlockSpec((tm, D_MODEL), lambda i, j: (i, 0)),
            mod.spec(layer, 1, 0, tm), mod.spec(layer, 1, 1, tm),
            mod.norm_spec(layer, 1),
            pl.BlockSpec((None, D_MODEL, tn), lambda i, j: (j, 0, 0)),
            pl.BlockSpec((D_MODEL, n_small), lambda i, j: (0, 0)),
        ],
        out_specs=[
            pl.BlockSpec((tn // LANE, tm, LANE), lambda i, j: (j, i, 0)),
            pl.BlockSpec((tm, n_small), lambda i, j: (i, 0)),
        ],
        out_shape=[jax.ShapeDtypeStruct((n_main // LANE, n, LANE), F32), jax.ShapeDtypeStruct((n, n_small), F32)],
        scratch_shapes=[pltpu.VMEM((tm, D_MODEL), BF16)],
        compiler_params=_cparams(("arbitrary", "arbitrary")),
        name="inproj",
    )(x, mod.table, mod.table, mod.norm_g, w_main, w_small)


def _outproj_kernel(x_ref, gt_ref, *refs):
    o_ref = refs[-1]
    npair = (len(refs) - 1) // 2
    y = _dot(refs[0][...], refs[npair][...])
    for p in range(1, npair):
        y = y + _dot(refs[p][...], refs[npair + p][...])
    o_ref[...] = x_ref[...] + gt_ref[0] * y


def _outproj(x, mod, layer, mixes, ws):
    n = x.shape[0]
    tm = TOKEN_TILE
    in_specs = [pl.BlockSpec((tm, D_MODEL), lambda i: (i, 0)), mod.spec(layer, 1, 2, tm)]
    in_specs += [pl.BlockSpec((tm, m.shape[1]), lambda i: (i, 0)) for m in mixes]
    in_specs += [pl.BlockSpec(w.shape, lambda i: (0, 0)) for w in ws]
    return pl.pallas_call(
        _outproj_kernel,
        grid=(n // tm,),
        in_specs=in_specs,
        out_specs=pl.BlockSpec((tm, D_MODEL), lambda i: (i, 0)),
        out_shape=jax.ShapeDtypeStruct((n, D_MODEL), F32),
        compiler_params=_cparams(("arbitrary",)),
        name="outproj",
    )(x, mod.table, *mixes, *ws)


def _scan_consts(c):
    nlev = int(math.log2(c))
    t = np.arange(c)
    row, col = t[:, None], t[None, :]
    blocks = [col <= row]
    masks = [np.eye(c, dtype=bool)]
    for lev in range(nlev):
        s = 1 << lev
        blk = t // s
        odd = (blk % 2) == 1
        bstart = (blk * s)[:, None]
        bend = bstart + s - 1
        as_query = (col >= bstart) & (col <= row)
        as_key = (col > row) & (col <= bend)
        blocks.append(np.where(odd[:, None], as_query, as_key))
        masks.append(odd[:, None] & (~odd[None, :]) & ((row // (2 * s)) == (col // (2 * s))))
    ones = np.ones((8, c), dtype=bool)
    m_f = np.concatenate(blocks + [ones], axis=0)
    m_b = np.concatenate([b[::-1, ::-1] for b in blocks] + [ones], axis=0)
    msum = jnp.asarray(np.stack([np.tile(m_f, (1, 2)), np.tile(m_b, (1, 2))]).astype(np.float32), dtype=BF16)
    mask = jnp.asarray(np.stack([np.stack(masks), np.stack([m[::-1, ::-1] for m in masks])]).astype(np.float32))
    return msum, mask, nlev


def _slabs(ref, rows):
    parts = [ref[s, rows, :] for s in range(ref.shape[0])]
    return parts[0] if len(parts) == 1 else jnp.concatenate(parts, axis=-1)


def _split2(g):
    hi = g.astype(BF16)
    lo = (g - hi.astype(F32)).astype(BF16)
    return hi, lo


def _scan_block(backward, gate_input, load_chunk, emit, msum_ref, mask_ref, st_ref, *, nch, c, nlev, state_first,
                nheads):
    msum = msum_ref[0]
    masks = [mask_ref[0, i] > 0.5 for i in range(nlev + 1)]
    starts = [((nch - 1 - ci) if backward else ci) * c for ci in range(nch)]
    order = [slice(r0, r0 + c) for r0 in starts]

    def head(a, s):
        w = a.shape[-1] // nheads
        return a[:, s * w:(s + 1) * w]

    def stage1(rows, z):
        q, k, v, g = load_chunk(rows, z)
        hi, lo = _split2(g * LOG2E)
        return q, k, v.astype(BF16), _dot(msum, jnp.concatenate([hi, lo], axis=0))

    def stage2(q, k, vb, x):
        e = jnp.exp2(x)
        tot = x[(1 + nlev) * c:(1 + nlev) * c + 1]
        dtot = e[(1 + nlev) * c:(1 + nlev) * c + 1]
        qe = (q * e[0:c]).astype(BF16)
        ke = (k * jnp.exp2(tot - x[0:c])).astype(BF16)
        qb, kb = q.astype(BF16), k.astype(BF16)
        eb = e[c:(1 + nlev) * c].astype(BF16)
        att = [jnp.where(masks[0], _dot_nt(head(qb, s), head(kb, s)), 0.0) for s in range(nheads)]
        for lev in range(nlev):
            es = eb[lev * c:(lev + 1) * c]
            qs, ks = qb * es, kb * es
            att = [jnp.where(masks[1 + lev], _dot_nt(head(qs, s), head(ks, s)), att[s]) for s in range(nheads)]
        inc = [_dot_tn(head(vb, s), head(ke, s)) for s in range(nheads)]
        return qe, dtot, [a.astype(BF16) for a in att], vb, inc

    def stage3(rows, st, qe, dtot, att, vb, inc):
        o = [_dot(att[s], head(vb, s)) + _dot_nt(head(qe, s), st[s].astype(BF16)) for s in range(nheads)]
        emit(rows, o[0] if nheads == 1 else jnp.concatenate(o, axis=-1))
        return [st[s] * head(dtot, s) + inc[s] for s in range(nheads)]

    zs, s1, s2 = {}, {}, {}
    st = [st_ref[s] for s in range(nheads)]
    for step in range(nch + 3):
        if step < nch:
            zs[step] = gate_input(order[step])
        if 0 <= step - 1 < nch:
            s1[step - 1] = stage1(order[step - 1], zs.pop(step - 1))
        if state_first and 0 <= step - 3 < nch:
            st = stage3(order[step - 3], st, *s2.pop(step - 3))
        if 0 <= step - 2 < nch:
            s2[step - 2] = stage2(*s1.pop(step - 2))
        if not state_first and 0 <= step - 3 < nch:
            st = stage3(order[step - 3], st, *s2.pop(step - 3))
    for s in range(nheads):
        st_ref[s] = st[s]


def _scan_both_directions(ph, blk, tb, gate_input, load_chunk, ofwd_ref, o_ref, on_ref, gate_ref, **kw):
    c, nheads = kw["c"], kw["nheads"]

    def seq_rows(rows):
        return pl.ds(pl.multiple_of(blk * tb + rows.start, c), c)

    def emit_fwd(rows, o):
        ofwd_ref[seq_rows(rows), :] = o

    def emit_bwd(rows, o):
        o_sum = ofwd_ref[seq_rows(rows), :] + o
        gate = _silu(_slabs(gate_ref, rows))
        dv = o_sum.shape[-1] // nheads
        for s in range(nheads):
            cols = slice(s * dv, (s + 1) * dv)
            os_ = o_sum[:, cols]
            ms = jnp.mean(os_ * os_, axis=-1, keepdims=True)
            o_ref[0, rows, cols] = (os_ * lax.rsqrt(ms + EPS) * on_ref[...] * gate[:, cols]).astype(BF16)

    pl.when(ph == 0)(lambda: _scan_block(False, gate_input, load_chunk, emit_fwd, **kw))
    pl.when(ph == 1)(lambda: _scan_block(True, gate_input, load_chunk, emit_bwd, **kw))


def _hgrn_kernel(q_ref, z_ref, v_ref, gate_ref, lb_ref, on_ref, msum_ref, mask_ref, o_ref, ofwd_ref, st_ref,
                 *, nblk, tb, c, nlev):
    ph = pl.program_id(2)
    jb = pl.program_id(3)
    blk = jnp.where(ph == 0, jb, nblk - 1 - jb)

    @pl.when(jb == 0)
    def _():
        st_ref[...] = jnp.zeros_like(st_ref)

    def lb_row(r):
        return jnp.concatenate([lb_ref[0, s, r:r + 1, :] for s in range(lb_ref.shape[1])], axis=-1)

    log_lb, log_1mlb, one_mlb = lb_row(0), lb_row(1), lb_row(2)

    def gate_input(rows):
        return _slabs(z_ref, rows)

    def load_chunk(rows, z):
        q = _silu(_slabs(q_ref, rows))
        t = jnp.exp(-jnp.abs(z))
        d = 1.0 + t
        a = log_lb
        b = log_1mlb + (jnp.minimum(z, 0.0) - jnp.log(d))
        g = jnp.maximum(a, b) + jnp.log(1.0 + jnp.exp(-jnp.abs(a - b)))
        k = one_mlb * (jnp.where(z > 0.0, t, 1.0) / d)
        return q, k, _slabs(v_ref, rows), g

    _scan_both_directions(ph, blk, tb, gate_input, load_chunk, ofwd_ref, o_ref, on_ref, gate_ref,
                          msum_ref=msum_ref, mask_ref=mask_ref, st_ref=st_ref, nch=tb // c, c=c, nlev=nlev,
                          state_first=False, nheads=HG_HEADS_PER_STEP)


def _hgrn_scan(u, lbp, onorm, b, t):
    tb, c = min(SCAN_BLOCK, t), CHUNK
    nblk = t // tb
    msum, mask, nlev = _scan_consts(c)
    hp = HG_HEADS_PER_STEP
    ng = HG_HEADS // hp

    def rows_of(bi, ph, jb):
        return bi * nblk + jnp.where(ph == 0, jb, nblk - 1 - jb)

    def gate_rows(bi, ph, jb):
        return bi * nblk + jnp.where(ph == 0, nblk - 1, nblk - 1 - jb)

    return pl.pallas_call(
        functools.partial(_hgrn_kernel, nblk=nblk, tb=tb, c=c, nlev=nlev),
        grid=(b, ng, 2, nblk),
        in_specs=[
            pl.BlockSpec((hp, tb, HG_D), lambda bi, h, ph, jb: (h, rows_of(bi, ph, jb), 0)),
            pl.BlockSpec((hp, tb, HG_D), lambda bi, h, ph, jb: (ng + ng * ph + h, rows_of(bi, ph, jb), 0)),
            pl.BlockSpec((hp, tb, HG_D), lambda bi, h, ph, jb: (3 * ng + h, rows_of(bi, ph, jb), 0)),
            pl.BlockSpec((hp, tb, HG_D), lambda bi, h, ph, jb: (4 * ng + h, gate_rows(bi, ph, jb), 0)),
            pl.BlockSpec((1, hp, 8, HG_D), lambda bi, h, ph, jb: (ph, h, 0, 0)),
            pl.BlockSpec((1, HG_D), lambda bi, h, ph, jb: (0, 0)),
            pl.BlockSpec((1,) + msum.shape[1:], lambda bi, h, ph, jb: (ph, 0, 0)),
            pl.BlockSpec((1,) + mask.shape[1:], lambda bi, h, ph, jb: (ph, 0, 0, 0)),
        ],
        out_specs=pl.BlockSpec(
            (1, tb, hp * HG_D), lambda bi, h, ph, jb: (bi, jnp.where(ph == 0, nblk - 1, nblk - 1 - jb), h)),
        out_shape=jax.ShapeDtypeStruct((b, t, HG_WIDTH), BF16),
        scratch_shapes=[pltpu.VMEM((t, hp * HG_D), F32), pltpu.VMEM((hp, HG_D, HG_D), F32)],
        compiler_params=_cparams(("arbitrary",) * 4),
        name="hgrn_scan",
    )(u, u, u, u, lbp, onorm, msum, mask)


def _gla_kernel(q_ref, k_ref, v_ref, gate_ref, r_ref, w2_ref, gb_ref, on_ref, msum_ref, mask_ref, o_ref,
                ofwd_ref, st_ref, *, nblk, tb, c, nlev):
    ph = pl.program_id(2)
    jb = pl.program_id(3)
    blk = jnp.where(ph == 0, jb, nblk - 1 - jb)

    @pl.when(jb == 0)
    def _():
        st_ref[...] = jnp.zeros_like(st_ref)

    w2 = w2_ref[0, 0]
    gb = gb_ref[0, 0]

    def gate_input(rows):
        return _dot(r_ref[0, rows, :].astype(BF16), w2) + gb

    def load_chunk(rows, z):
        q = _slabs(q_ref, rows) * (GLA_DK ** -0.5)
        g = _log_sigmoid(z) * (1.0 / GLA_GATE_NORM)
        return q, _slabs(k_ref, rows), _slabs(v_ref, rows), g

    _scan_both_directions(ph, blk, tb, gate_input, load_chunk, ofwd_ref, o_ref, on_ref, gate_ref,
                          msum_ref=msum_ref, mask_ref=mask_ref, st_ref=st_ref, nch=tb // c, c=c, nlev=nlev,
                          state_first=True, nheads=1)


def _gla_scan(u, r, w2p, gbias, onorm, b, t):
    tb, c = min(SCAN_BLOCK, t), CHUNK
    nblk = t // tb
    msum, mask, nlev = _scan_consts(c)
    nh = GLA_HEADS
    ks, vs = GLA_DK // LANE, GLA_DV // LANE
    vblk0 = 2 * nh * ks // vs

    def blk_of(ph, jb):
        return jnp.where(ph == 0, jb, nblk - 1 - jb)

    def rows_of(bi, ph, jb):
        return bi * nblk + blk_of(ph, jb)

    def gate_rows(bi, ph, jb):
        return bi * nblk + jnp.where(ph == 0, nblk - 1, nblk - 1 - jb)

    return pl.pallas_call(
        functools.partial(_gla_kernel, nblk=nblk, tb=tb, c=c, nlev=nlev),
        grid=(b, nh, 2, nblk),
        in_specs=[
            pl.BlockSpec((ks, tb, LANE), lambda bi, h, ph, jb: (h, rows_of(bi, ph, jb), 0)),
            pl.BlockSpec((ks, tb, LANE), lambda bi, h, ph, jb: (nh + h, rows_of(bi, ph, jb), 0)),
            pl.BlockSpec((vs, tb, LANE), lambda bi, h, ph, jb: (vblk0 + h, rows_of(bi, ph, jb), 0)),
            pl.BlockSpec((vs, tb, LANE), lambda bi, h, ph, jb: (vblk0 + nh + h, gate_rows(bi, ph, jb), 0)),
            pl.BlockSpec((1, tb, GLA_R_PAD), lambda bi, h, ph, jb: (bi, blk_of(ph, jb), 0)),
            pl.BlockSpec((1, 1, GLA_R_PAD, GLA_DK), lambda bi, h, ph, jb: (ph, h, 0, 0)),
            pl.BlockSpec((1, 1, 1, GLA_DK), lambda bi, h, ph, jb: (ph, h, 0, 0)),
            pl.BlockSpec((1, GLA_DV), lambda bi, h, ph, jb: (0, 0)),
            pl.BlockSpec((1,) + msum.shape[1:], lambda bi, h, ph, jb: (ph, 0, 0)),
            pl.BlockSpec((1,) + mask.shape[1:], lambda bi, h, ph, jb: (ph, 0, 0, 0)),
        ],
        out_specs=pl.BlockSpec(
            (1, tb, GLA_DV), lambda bi, h, ph, jb: (bi, jnp.where(ph == 0, nblk - 1, nblk - 1 - jb), h)),
        out_shape=jax.ShapeDtypeStruct((b, t, nh * GLA_DV), BF16),
        scratch_shapes=[pltpu.VMEM((t, GLA_DV), F32), pltpu.VMEM((1, GLA_DV, GLA_DK), F32)],
        compiler_params=_cparams(("arbitrary",) * 4),
        name="gla_scan",
    )(u, u, u, u, r, w2p, gbias, onorm, msum, mask)


def _rope128(x, cos, sin):
    lane = lax.broadcasted_iota(jnp.int32, x.shape, 1)
    half = MLA_ROPE // 2
    swapped = jnp.where(lane < half, pltpu.roll(x, LANE - half, 1), pltpu.roll(x, half, 1))
    return x * cos + swapped * sin


def _mla_prep_kernel(c_ref, cos_ref, sin_ref, qag_ref, kvag_ref, wq_ref, wk_ref, wv_ref, gq_ref, gk_ref,
                     q_ref, k_ref, v_ref):
    c = c_ref[...]
    cq = c[:, :Q_LORA]
    ckv = c[:, Q_LORA:Q_LORA + KV_LORA]
    kpe = c[:, Q_LORA + KV_LORA:]
    cqn = (cq * lax.rsqrt(jnp.mean(cq * cq, axis=-1, keepdims=True) + EPS) * qag_ref[...]).astype(BF16)
    ckvn = (ckv * lax.rsqrt(jnp.mean(ckv * ckv, axis=-1, keepdims=True) + EPS) * kvag_ref[...]).astype(BF16)
    cos = cos_ref[...]
    sin = sin_ref[...]
    gq = gq_ref[...]
    gk = gk_ref[...]
    q_raw = _dot(cqn, wq_ref[...])
    kn_raw = _dot(ckvn, wk_ref[...])
    v_ref[...] = _dot(ckvn, wv_ref[...]).astype(BF16)
    kpe_ss = jnp.sum(kpe * kpe, axis=-1, keepdims=True)
    kpe_rot = _rope128(kpe * gk[:, LANE:], cos, sin)
    for h in range(MLA_HEADS):
        lo = h * MLA_QK_PAD
        qn = q_raw[:, lo:lo + LANE]
        qr = q_raw[:, lo + LANE:lo + 2 * LANE]
        ss = jnp.sum(qn * qn, axis=-1, keepdims=True) + jnp.sum(qr * qr, axis=-1, keepdims=True)
        rinv = lax.rsqrt(ss * (1.0 / MLA_QK) + EPS) * ATTN_Q_SCALE
        q_ref[:, lo:lo + LANE] = (qn * rinv * gq[:, :LANE]).astype(BF16)
        q_ref[:, lo + LANE:lo + 2 * LANE] = _rope128(qr * rinv * gq[:, LANE:], cos, sin).astype(BF16)
        kn = kn_raw[:, h * LANE:(h + 1) * LANE]
        ssk = jnp.sum(kn * kn, axis=-1, keepdims=True) + kpe_ss
        rinvk = lax.rsqrt(ssk * (1.0 / MLA_QK) + EPS)
        k_ref[:, lo:lo + LANE] = (kn * rinvk * gk[:, :LANE]).astype(BF16)
        k_ref[:, lo + LANE:lo + 2 * LANE] = (kpe_rot * rinvk).astype(BF16)


def _mla_prep(c, cos, sin, qag, kvag, wq, wk, wv, gq, gk, t):
    n = c.shape[0]
    tm = min(256, t)
    tps = t // tm
    full = lambda a: pl.BlockSpec(a.shape, lambda i: (0,) * a.ndim)
    return pl.pallas_call(
        _mla_prep_kernel,
        grid=(n // tm,),
        in_specs=[
            pl.BlockSpec((tm, MLA_C_PAD), lambda i: (i, 0)),
            pl.BlockSpec((tm, LANE), lambda i: (i % tps, 0)),
            pl.BlockSpec((tm, LANE), lambda i: (i % tps, 0)),
            full(qag), full(kvag), full(wq), full(wk), full(wv), full(gq), full(gk),
        ],
        out_specs=[
            pl.BlockSpec((tm, MLA_HEADS * MLA_QK_PAD), lambda i: (i, 0)),
            pl.BlockSpec((tm, MLA_HEADS * MLA_QK_PAD), lambda i: (i, 0)),
            pl.BlockSpec((tm, MLA_HEADS * MLA_V), lambda i: (i, 0)),
        ],
        out_shape=[
            jax.ShapeDtypeStruct((n, MLA_HEADS * MLA_QK_PAD), BF16),
            jax.ShapeDtypeStruct((n, MLA_HEADS * MLA_QK_PAD), BF16),
            jax.ShapeDtypeStruct((n, MLA_HEADS * MLA_V), BF16),
        ],
        compiler_params=_cparams(("arbitrary",)),
        name="mla_prep",
    )(c, cos, sin, qag, kvag, wq, wk, wv, gq, gk)


def _attn_kernel(q_ref, k_ref, v_ref, o_ref, *, tk, nk, nsplit):
    tq = q_ref.shape[1]
    th = tq // nsplit
    q = q_ref[0]

    def scores(ci):
        s = _dot_nt(q, k_ref[0, ci * tk:(ci + 1) * tk, :])
        return [s[i * th:(i + 1) * th, :] for i in range(nsplit)]

    def update(ci, s_list, state):
        vc = v_ref[0, ci * tk:(ci + 1) * tk, :]
        out = []
        for s, (m, l, acc) in zip(s_list, state):
            m_new = jnp.maximum(m, jnp.max(s, axis=-1, keepdims=True))
            alpha = jnp.exp2(m - m_new)
            p = jnp.exp2(s - m_new)
            l = alpha * l + jnp.sum(p, axis=-1, keepdims=True)
            acc = alpha * acc + _dot(p.astype(BF16), vc)
            out.append((m_new, l, acc))
        return out

    state = [(jnp.full((th, 1), -jnp.inf, F32), jnp.zeros((th, 1), F32), jnp.zeros((th, MLA_V), F32))
             for _ in range(nsplit)]
    s_cur = scores(0)
    for ci in range(nk):
        s_next = scores(ci + 1) if ci + 1 < nk else None
        state = update(ci, s_cur, state)
        s_cur = s_next
    for i, (_, l, acc) in enumerate(state):
        o_ref[0, i * th:(i + 1) * th, :] = (acc / l).astype(BF16)


def _attention(q, k, v, b, t):
    tq, tk = min(ATTN_TQ, t), min(ATTN_TK, t)
    return pl.pallas_call(
        functools.partial(_attn_kernel, tk=tk, nk=t // tk, nsplit=ATTN_SPLIT),
        grid=(b, MLA_HEADS, t // tq),
        in_specs=[
            pl.BlockSpec((1, tq, MLA_QK_PAD), lambda bi, h, i: (bi, i, h)),
            pl.BlockSpec((1, t, MLA_QK_PAD), lambda bi, h, i: (bi, 0, h)),
            pl.BlockSpec((1, t, MLA_V), lambda bi, h, i: (bi, 0, h)),
        ],
        out_specs=pl.BlockSpec((1, tq, MLA_V), lambda bi, h, i: (bi, i, h)),
        out_shape=jax.ShapeDtypeStruct((b, t, MLA_HEADS * MLA_V), BF16),
        compiler_params=_cparams(("arbitrary",) * 3),
        name="mla_attention",
    )(q, k, v)


def _rope_tables(t):
    half = MLA_ROPE // 2
    inv_freq = ROPE_THETA ** (-jnp.arange(half, dtype=F32) / half)
    ang = jnp.arange(t, dtype=jnp.int32).astype(F32)[:, None] * inv_freq[None, :]
    cos, sin = jnp.cos(ang), jnp.sin(ang)
    zeros = jnp.zeros((t, LANE - MLA_ROPE), F32)
    return (jnp.concatenate([cos, cos, zeros], axis=-1), jnp.concatenate([-sin, sin, zeros], axis=-1))


def _pad_cols(a, width):
    return jnp.pad(a, [(0, 0)] * (a.ndim - 1) + [(0, width - a.shape[-1])])


def _tile_cols(w, tn):
    k, n = w.shape[-2:]
    lead = w.shape[:-2]
    w = w.reshape(lead + (k, n // tn, tn))
    return jnp.swapaxes(w, -3, -2)


def _prep_even(e, w):
    ev_in = w['ev_w_in'][e]
    w_hg = _tile_cols(ev_in[:, :5 * HG_WIDTH].astype(BF16), INPROJ_TILE)
    w_mla = _pad_cols(ev_in[:, 5 * HG_WIDTH:], MLA_C_PAD).astype(BF16)
    uq = w['mla_w_uq'][e].reshape(Q_LORA, MLA_HEADS, MLA_QK)
    wq = _pad_cols(uq, MLA_QK_PAD).reshape(Q_LORA, MLA_HEADS * MLA_QK_PAD).astype(BF16)
    ukv = w['mla_w_ukv'][e].reshape(KV_LORA, MLA_HEADS, MLA_NOPE + MLA_V)
    wk = ukv[:, :, :MLA_NOPE].reshape(KV_LORA, MLA_HEADS * MLA_NOPE).astype(BF16)
    wv = ukv[:, :, MLA_NOPE:].reshape(KV_LORA, MLA_HEADS * MLA_V).astype(BF16)
    p = jax.nn.softmax(w['hgrn_lb'].astype(F32), axis=1)
    lb = jnp.cumsum(p, axis=1)
    lb = (lb - lb[:, :1])[:, e].reshape(2, HG_HEADS, 1, HG_D)
    lbp = jnp.concatenate([jnp.log(lb), jnp.log1p(-lb), 1.0 - lb, jnp.zeros((2, HG_HEADS, 5, HG_D), F32)], axis=2)
    w_out = w['ev_w_out'][e].astype(BF16)
    return dict(
        w_hg=w_hg, w_mla=w_mla, wq=wq, wk=wk, wv=wv, lbp=lbp,
        hg_onorm=w['hgrn_onorm_g'][e].reshape(1, HG_D),
        qag=w['mla_qa_norm_g'][e].reshape(1, Q_LORA), kvag=w['mla_kva_norm_g'][e].reshape(1, KV_LORA),
        gq=_pad_cols(w['mla_qn_g'][e].reshape(1, MLA_QK), MLA_QK_PAD),
        gk=_pad_cols(w['mla_kn_g'][e].reshape(1, MLA_QK), MLA_QK_PAD),
        w_out_hg=w_out[:HG_WIDTH], w_out_mla=w_out[HG_WIDTH:],
    )


def _prep_odd(e, w):
    od_in = w['od_w_in'][e]
    w_main = _tile_cols(od_in[:, :GLA_MAIN].astype(BF16), INPROJ_TILE)
    w_r = _pad_cols(od_in[:, GLA_MAIN:], GLA_R_PAD).astype(BF16)
    w2 = w['gla_gk_w2'][e].reshape(2, GLA_RANK, GLA_HEADS, GLA_DK).transpose(0, 2, 1, 3)
    w2p = jnp.zeros((2, GLA_HEADS, GLA_R_PAD, GLA_DK), F32)
    w2p = w2p.at[0, :, :GLA_RANK].set(w2[0]).at[1, :, GLA_RANK:2 * GLA_RANK].set(w2[1]).astype(BF16)
    gbias = w['gla_gk_b'][e].reshape(2, GLA_HEADS, 1, GLA_DK)
    return dict(w_main=w_main, w_r=w_r, w2p=w2p, gbias=gbias,
                onorm=w['gla_onorm_g'][e].reshape(1, GLA_DV), w_out=w['od_w_out'][e].astype(BF16))


def _trunk(x3, boff, nb_total, table, norm_g, w13, w2, evens, odds):
    b, t, _ = x3.shape
    n = b * t
    x = x3.reshape(n, D_MODEL)
    mod = _Mod(table, norm_g, nb_total, boff, t)
    cos, sin = _rope_tables(t)
    for layer in range(DEPTH):
        x = _ffn(x, mod, layer, 0, 0, w13, w2)
        e = layer // 2
        if layer % 2 == 0:
            p = evens[e]
            u_hg, c_mla = _inproj(x, mod, layer, p['w_hg'], p['w_mla'])
            o_hg = _hgrn_scan(u_hg, p['lbp'], p['hg_onorm'], b, t)
            q, k, v = _mla_prep(c_mla, cos, sin, p['qag'], p['kvag'], p['wq'], p['wk'], p['wv'], p['gq'], p['gk'], t)
            o_mla = _attention(q.reshape(b, t, -1), k.reshape(b, t, -1), v.reshape(b, t, -1), b, t)
            x = _outproj(x, mod, layer, [o_hg.reshape(n, HG_WIDTH), o_mla.reshape(n, MLA_HEADS * MLA_V)],
                         [p['w_out_hg'], p['w_out_mla']])
        else:
            p = odds[e]
            u, r = _inproj(x, mod, layer, p['w_main'], p['w_r'])
            o = _gla_scan(u, r.reshape(b, t, GLA_R_PAD), p['w2p'], p['gbias'], p['onorm'], b, t)
            x = _outproj(x, mod, layer, [o.reshape(n, GLA_HEADS * GLA_DV)], [p['w_out']])
        x = _ffn(x, mod, layer, 2, 1, w13, w2)
    return x.reshape(b, t, D_MODEL)


def kernel(x_prompt, x_sample, c_prompt, c_sample, ada_w, ada_b, norm_g, ffn_w13, ffn_w2, ev_w_in, ev_w_out, hgrn_lb, hgrn_onorm_g, mla_qa_norm_g, mla_w_uq, mla_kva_norm_g, mla_w_ukv, mla_qn_g, mla_kn_g, od_w_in, od_w_out, gla_gk_w2, gla_gk_b, gla_onorm_g):
    w = dict(ev_w_in=ev_w_in, ev_w_out=ev_w_out, hgrn_lb=hgrn_lb, hgrn_onorm_g=hgrn_onorm_g,
             mla_qa_norm_g=mla_qa_norm_g, mla_w_uq=mla_w_uq, mla_kva_norm_g=mla_kva_norm_g,
             mla_w_ukv=mla_w_ukv, mla_qn_g=mla_qn_g, mla_kn_g=mla_kn_g, od_w_in=od_w_in,
             od_w_out=od_w_out, gla_gk_w2=gla_gk_w2, gla_gk_b=gla_gk_b, gla_onorm_g=gla_onorm_g)
    bp, bs = x_prompt.shape[0], x_sample.shape[0]
    nb_total = bp + bs
    c_all = jnp.concatenate([c_prompt, c_sample], axis=0)
    mod = _ada_mod(c_all, ada_w, ada_b)
    table = mod.reshape(DEPTH * nb_total * 3 * N_SUB, 1, D_MODEL)
    ng = norm_g.reshape(DEPTH * N_SUB, 1, D_MODEL)
    w13 = ffn_w13.astype(BF16)
    w2 = ffn_w2.astype(BF16)
    evens = [_prep_even(e, w) for e in range((DEPTH + 1) // 2)]
    odds = [_prep_odd(e, w) for e in range(DEPTH // 2)]
    y_prompt = _trunk(x_prompt, 0, nb_total, table, ng, w13, w2, evens, odds)
    y_sample = _trunk(x_sample, bp, nb_total, table, ng, w13, w2, evens, odds)
    return (y_prompt, y_sample)
```
